```python
import jax, jax.numpy as jnp
from jax import lax
import numpy as np

D_MODEL = 4096
BATCH = 16
SEQ = 256
DEPTH = 2
DEC_BATCH = 4
DEC_SEQ = 4096
PAST_LEN = 512

GRID_W = 64
D_MIX = D_MODEL
D_RWKV = D_MIX // 2
D_CONV = D_MIX - D_RWKV
HEAD_DIM = 64
H_RWKV = D_RWKV // HEAD_DIM
LORA_W = 96
LORA_A = 96
CONV_K = 31
D_SHIFT = 3 * D_RWKV + LORA_W + LORA_A
D_IN = D_SHIFT + D_RWKV + 2 * D_CONV + D_CONV
NORM_EPS = 1e-6
GN_EPS = 64e-5
LN_EPS = 1e-5

kernel_name = "hybrid_rwkv7_conformer_diffusion_step"


def rmsnorm(x, g):
    x32 = x.astype(jnp.float32)
    return x32 * lax.rsqrt(jnp.mean(x32 * x32, axis=-1, keepdims=True) + NORM_EPS) * g


def shift_1d(u):
    half = u.shape[-1] // 2
    prev = jnp.pad(u[:, :-1, :half], ((0, 0), (1, 0), (0, 0)))
    nxt = jnp.pad(u[:, 1:, half:], ((0, 0), (0, 1), (0, 0)))
    return jnp.concatenate([prev, nxt], axis=-1)


def shift_grid(u):
    B, T, C = u.shape
    rows = T // GRID_W
    q = C // 4
    g = u.reshape(B, rows, GRID_W, C)
    left = jnp.pad(g[:, :, :-1, :q], ((0, 0), (0, 0), (1, 0), (0, 0)))
    right = jnp.pad(g[:, :, 1:, q:2 * q], ((0, 0), (0, 0), (0, 1), (0, 0)))
    up = jnp.pad(g[:, :-1, :, 2 * q:3 * q], ((0, 0), (1, 0), (0, 0), (0, 0)))
    down = jnp.pad(g[:, 1:, :, 3 * q:], ((0, 0), (0, 1), (0, 0), (0, 0)))
    return jnp.concatenate([left, right, up, down], axis=-1).reshape(B, T, C)


def rwkv7_scan(r, decay, k, v, kk, a, S0, reverse):
    xs = tuple(jnp.moveaxis(t, 1, 0) for t in (r, decay, k, v, kk, a))

    def step(S, inp):
        r_t, w_t, k_t, v_t, kk_t, a_t = inp
        sa = jnp.einsum('bhvk,bhk->bhv', S, -kk_t)
        S = (S * w_t[:, :, None, :]
             + sa[..., None] * (kk_t * a_t)[:, :, None, :]
             + v_t[..., None] * k_t[:, :, None, :])
        o_t = jnp.einsum('bhvk,bhk->bhv', S, r_t)
        return S, o_t

    S_final, o = lax.scan(step, S0, xs, reverse=reverse)
    return jnp.moveaxis(o, 0, 1), S_final


def mixer(h, S0_f, S0_b, shift_fn, w_in, mu_shift, decay_w0, decay_w2, iclr_a0, iclr_a2,
          k_k, k_a, r_k, lnx_g, lnx_b, conv_w, conv_b, cln_g, cln_b, w_out):
    B, T, _ = h.shape
    proj = jnp.einsum('btd,de->bte', h, w_in)
    u, g_rwkv, glu_in, g_conv = jnp.split(
        proj, [D_SHIFT, D_SHIFT + D_RWKV, D_SHIFT + D_RWKV + 2 * D_CONV], axis=-1)

    u = u + mu_shift * (shift_fn(u) - u)
    r, k, v, xw, xa = jnp.split(
        u, [D_RWKV, 2 * D_RWKV, 3 * D_RWKV, 3 * D_RWKV + LORA_W], axis=-1)
    heads = lambda t: t.reshape(B, T, H_RWKV, HEAD_DIM)
    kk = heads(k * k_k)
    kk = kk * lax.rsqrt(jnp.maximum(jnp.sum(kk * kk, axis=-1, keepdims=True), 1e-24))
    tw = jnp.tanh(xw)
    rh, vh = heads(r), heads(v)
    o_sum = jnp.zeros_like(rh)
    bonus = jnp.zeros_like(rh)
    finals = []
    for d, (S0, rev) in enumerate(((S0_f, False), (S0_b, True))):
        wlog = -jax.nn.softplus(-(decay_w0[d] + tw @ decay_w2[d])) - 0.5
        decay = jnp.exp(-jnp.exp(wlog))
        a = jax.nn.sigmoid(iclr_a0[d] + xa @ iclr_a2[d])
        kt = heads(k * (1.0 + (a - 1.0) * k_a))
        o_d, S_d = rwkv7_scan(rh, heads(decay), kt, vh, kk, heads(a),
                              S0.astype(jnp.float32), rev)
        o_sum = o_sum + o_d
        bonus = bonus + jnp.sum(rh * kt * r_k, axis=-1, keepdims=True) * vh
        finals.append(S_d)
    mu = jnp.mean(o_sum, axis=-1, keepdims=True)
    var = jnp.mean(jnp.square(o_sum - mu), axis=-1, keepdims=True)
    o_n = ((o_sum - mu) * lax.rsqrt(var + GN_EPS)).reshape(B, T, D_RWKV) * lnx_g + lnx_b
    x_rwkv = (o_n + bonus.reshape(B, T, D_RWKV)) * jax.nn.silu(g_rwkv)

    p, q = jnp.split(glu_in, 2, axis=-1)
    z = (p * jax.nn.sigmoid(q)).astype(jnp.float32)
    z = lax.conv_general_dilated(
        z, conv_w.astype(jnp.float32)[:, None, :], window_strides=(1,),
        padding=[(CONV_K // 2, CONV_K // 2)], dimension_numbers=('NWC', 'WIO', 'NWC'),
        feature_group_count=D_CONV) + conv_b
    zm = jnp.mean(z, axis=-1, keepdims=True)
    zv = jnp.mean(jnp.square(z - zm), axis=-1, keepdims=True)
    z = (z - zm) * lax.rsqrt(zv + LN_EPS) * cln_g + cln_b
    x_conv = jax.nn.silu(z) * jax.nn.silu(g_conv)

    y = jnp.einsum('btc,cd->btd', jnp.concatenate([x_rwkv, x_conv], axis=-1), w_out)
    return y, finals[0], finals[1]


def setup_inputs(seed: int = 0) -> dict:
    key = jax.random.key(seed)
    ks = jax.random.split(key, 32)
    nrm = lambda k, shape, s: s * jax.random.normal(k, shape, jnp.float32)
    return {
        "x_prompt": nrm(ks[0], (BATCH, SEQ, D_MODEL), 1.0),
        "x_sample": nrm(ks[1], (DEC_BATCH, DEC_SEQ, D_MODEL), 1.0),
        "state_rwkv": nrm(ks[2], (DEC_BATCH, DEPTH, 2, H_RWKV, HEAD_DIM, HEAD_DIM), 0.3),
        "c": nrm(ks[3], (DEC_BATCH, D_MODEL), 1.0),
        "c_ctx": nrm(ks[4], (D_MODEL,), 1.0),
        "norm_g": 1.0 + nrm(ks[5], (DEPTH, D_MODEL), 0.05),
        "ada_w": nrm(ks[6], (DEPTH, D_MODEL, 3 * D_MODEL), D_MODEL ** -0.5),
        "ada_b": nrm(ks[7], (DEPTH, 3 * D_MODEL), 0.02),
        "w_in": nrm(ks[8], (DEPTH, D_MODEL, D_IN), D_MODEL ** -0.5),
        "mu_shift": 0.5 + nrm(ks[9], (DEPTH, D_SHIFT), 0.1),
        "decay_w0": nrm(ks[10], (DEPTH, 2, D_RWKV), 0.5),
        "decay_w2": nrm(ks[11], (DEPTH, 2, LORA_W, D_RWKV), 0.1),
        "iclr_a0": nrm(ks[12], (DEPTH, 2, D_RWKV), 0.5),
        "iclr_a2": nrm(ks[13], (DEPTH, 2, LORA_A, D_RWKV), 0.1),
        "k_k": 0.85 + nrm(ks[14], (DEPTH, D_RWKV), 0.05),
        "k_a": 1.0 + nrm(ks[15], (DEPTH, D_RWKV), 0.05),
        "r_k": nrm(ks[16], (DEPTH, H_RWKV, HEAD_DIM), 0.1),
        "lnx_g": 1.0 + nrm(ks[17], (DEPTH, D_RWKV), 0.05),
        "lnx_b": nrm(ks[18], (DEPTH, D_RWKV), 0.02),
        "conv_w": nrm(ks[19], (DEPTH, CONV_K, D_CONV), CONV_K ** -0.5),
        "conv_b": nrm(ks[20], (DEPTH, D_CONV), 0.02),
        "cln_g": 1.0 + nrm(ks[21], (DEPTH, D_CONV), 0.05),
        "cln_b": nrm(ks[22], (DEPTH, D_CONV), 0.02),
        "w_out": nrm(ks[23], (DEPTH, D_MIX, D_MODEL), D_MIX ** -0.5),
        "final_g": 1.0 + nrm(ks[24], (D_MODEL,), 0.05),
    }


def reference(x_prompt, x_sample, state_rwkv, c, c_ctx, norm_g, ada_w, ada_b, w_in,
              mu_shift, decay_w0, decay_w2, iclr_a0, iclr_a2, k_k, k_a, r_k, lnx_g,
              lnx_b, conv_w, conv_b, cln_g, cln_b, w_out, final_g):
    x_ctx = x_prompt
    x_lat = x_sample
    cond_ctx = jax.nn.silu(c_ctx)[None, None, :]
    cond_lat = jax.nn.silu(c)[:, None, :]
    zeros_state = jnp.zeros((x_prompt.shape[0], H_RWKV, HEAD_DIM, HEAD_DIM), jnp.float32)
    new_states = []
    for l in range(DEPTH):
        lw = (w_in[l], mu_shift[l], decay_w0[l], decay_w2[l], iclr_a0[l], iclr_a2[l],
              k_k[l], k_a[l], r_k[l], lnx_g[l], lnx_b[l], conv_w[l], conv_b[l],
              cln_g[l], cln_b[l], w_out[l])
        mod = jnp.einsum('...d,de->...e', cond_ctx, ada_w[l]) + ada_b[l]
        sh, sc, gt = jnp.split(mod, 3, axis=-1)
        h = rmsnorm(x_ctx, norm_g[l]) * (1.0 + sc) + sh
        y, S_f, S_b = mixer(h, zeros_state, zeros_state, shift_1d, *lw)
        x_ctx = x_ctx + gt * y
        new_states.append(jnp.stack([S_f, S_b], axis=1))
        mod = jnp.einsum('...d,de->...e', cond_lat, ada_w[l]) + ada_b[l]
        sh, sc, gt = jnp.split(mod, 3, axis=-1)
        h = rmsnorm(x_lat, norm_g[l]) * (1.0 + sc) + sh
        y, _, _ = mixer(h, state_rwkv[:, l, 0], state_rwkv[:, l, 1], shift_grid, *lw)
        x_lat = x_lat + gt * y
    y_prompt = rmsnorm(x_ctx, final_g)
    y_sample = rmsnorm(x_lat, final_g)
    new_state_rwkv = jnp.stack(new_states, axis=1)
    return (y_prompt, y_sample, new_state_rwkv)
```

```python
import functools

import numpy as np
import jax
import jax.numpy as jnp
from jax import lax
from jax.experimental import pallas as pl
from jax.experimental.pallas import tpu as pltpu

F32 = jnp.float32
BF16 = jnp.bfloat16

HEAD_DIM = 64
GRID_W = 64
LORA_PAD = 256
GROUP_HEADS = 4
GROUP_W = GROUP_HEADS * HEAD_DIM
OUT_BLK = 32
NORM_EPS = 1e-6
GN_EPS = 64e-5
LN_EPS = 1e-5
EXP_M05 = float(np.exp(-0.5))
VMEM_LIMIT_BYTES = 56 * 1024 * 1024


def _cparams(sem):
    return pltpu.CompilerParams(dimension_semantics=sem, vmem_limit_bytes=VMEM_LIMIT_BYTES)


def _silu(x):
    return x * jax.nn.sigmoid(x)


def _bdot(a, b):
    return jnp.dot(a.astype(BF16), b.astype(BF16), preferred_element_type=F32)


def _mod_kernel(c_ref, w_ref, b_ref, o_ref):
    o_ref[...] = _bdot(_silu(c_ref[...]), w_ref[...]) + b_ref[...]


def _modulation(cond8, ada_w, ada_b):
    depth, d, n3 = ada_w.shape
    tn = min(512, n3)
    return pl.pallas_call(
        _mod_kernel,
        grid=(depth, n3 // tn),
        in_specs=[
            pl.BlockSpec((8, d), lambda l, j: (0, 0)),
            pl.BlockSpec((None, d, tn), lambda l, j: (l, 0, j)),
            pl.BlockSpec((None, 1, tn), lambda l, j: (l, 0, j)),
        ],
        out_specs=pl.BlockSpec((None, 8, tn), lambda l, j: (l, 0, j)),
        out_shape=jax.ShapeDtypeStruct((depth, 8, n3), F32),
        compiler_params=_cparams(("parallel", "parallel")),
        name="modulation",
    )(cond8, ada_w, ada_b.reshape(depth, 1, n3))


def _norm_mod_kernel(x_ref, g_ref, mod_ref, h_ref, *, d, per_batch_row):
    row = (1 + pl.program_id(0)) if per_batch_row else 0
    x = x_ref[...]
    xn = x * lax.rsqrt(jnp.mean(x * x, axis=-1, keepdims=True) + NORM_EPS) * g_ref[...]
    m = mod_ref[pl.ds(row, 1), :]
    h_ref[...] = (xn * (1.0 + m[:, d:2 * d]) + m[:, :d]).astype(h_ref.dtype)


def _norm_mod(x, g, mod, per_batch_row):
    b, t, d = x.shape
    tt = min(256, t)
    return pl.pallas_call(
        functools.partial(_norm_mod_kernel, d=d, per_batch_row=per_batch_row),
        grid=(b, t // tt),
        in_specs=[
            pl.BlockSpec((None, tt, d), lambda i, j: (i, j, 0)),
            pl.BlockSpec((1, d), lambda i, j: (0, 0)),
            pl.BlockSpec((8, 3 * d), lambda i, j: (0, 0)),
        ],
        out_specs=pl.BlockSpec((None, tt, d), lambda i, j: (i, j, 0)),
        out_shape=jax.ShapeDtypeStruct((b, t, d), BF16),
        compiler_params=_cparams(("parallel", "parallel")),
        name="norm_mod",
    )(x, g.reshape(1, d), mod)


def _final_norm_kernel(x_ref, g_ref, o_ref):
    x = x_ref[...]
    o_ref[...] = x * lax.rsqrt(jnp.mean(x * x, axis=-1, keepdims=True) + NORM_EPS) * g_ref[...]


def _final_norm(x, g):
    b, t, d = x.shape
    tt = min(256, t)
    return pl.pallas_call(
        _final_norm_kernel,
        grid=(b, t // tt),
        in_specs=[
            pl.BlockSpec((None, tt, d), lambda i, j: (i, j, 0)),
            pl.BlockSpec((1, d), lambda i, j: (0, 0)),
        ],
        out_specs=pl.BlockSpec((None, tt, d), lambda i, j: (i, j, 0)),
        out_shape=jax.ShapeDtypeStruct((b, t, d), F32),
        compiler_params=_cparams(("parallel", "parallel")),
        name="final_norm",
    )(x, g.reshape(1, d))


def _mm_kernel(a_ref, w_ref, o_ref):
    o_ref[...] = jnp.dot(a_ref[...], w_ref[...], preferred_element_type=F32)


def _pick_tile(n, unit, cap):
    best = unit
    k = 1
    while k * unit <= min(n, cap):
        if n % (k * unit) == 0:
            best = k * unit
        k += 1
    return best


def _in_proj(h, w):
    m, k = h.shape
    n = w.shape[1]
    tm = _pick_tile(m, 256, 1024)
    tn = _pick_tile(n, 256, 768)
    return pl.pallas_call(
        _mm_kernel,
        grid=(m // tm, n // tn),
        in_specs=[
            pl.BlockSpec((tm, k), lambda i, j: (i, 0)),
            pl.BlockSpec((k, tn), lambda i, j: (0, j)),
        ],
        out_specs=pl.BlockSpec((tm, tn), lambda i, j: (i, j)),
        out_shape=jax.ShapeDtypeStruct((m, n), F32),
        compiler_params=_cparams(("parallel", "arbitrary")),
        name="in_proj",
    )(h, w)


def _out_proj_kernel(a_ref, w_ref, x_ref, gt_ref, o_ref, *, rows_per_batch, tm):
    if rows_per_batch is None:
        row = 0
    else:
        row = 1 + (pl.program_id(0) * tm) // rows_per_batch
    y = jnp.dot(a_ref[...], w_ref[...], preferred_element_type=F32)
    o_ref[...] = x_ref[...] + gt_ref[pl.ds(row, 1), :] * y


def _out_proj(a, w, x, gt8, rows_per_batch):
    m, k = a.shape
    n = w.shape[1]
    tm = _pick_tile(m, 256, 512)
    if rows_per_batch is not None:
        assert rows_per_batch % tm == 0
    tn = _pick_tile(n, 256, 1024)
    return pl.pallas_call(
        functools.partial(_out_proj_kernel, rows_per_batch=rows_per_batch, tm=tm),
        grid=(m // tm, n // tn),
        in_specs=[
            pl.BlockSpec((tm, k), lambda i, j: (i, 0)),
            pl.BlockSpec((k, tn), lambda i, j: (0, j)),
            pl.BlockSpec((tm, tn), lambda i, j: (i, j)),
            pl.BlockSpec((8, tn), lambda i, j: (0, j)),
        ],
        out_specs=pl.BlockSpec((tm, tn), lambda i, j: (i, j)),
        out_shape=jax.ShapeDtypeStruct((m, n), F32),
        compiler_params=_cparams(("parallel", "arbitrary")),
        name="out_proj",
    )(a, w, x, gt8)


def _shift_sources(seg_start, seg_width, d_shift, grid_mode):
    if grid_mode:
        q = d_shift // 4
        bounds = [(-1, 0, q), (1, q, 2 * q), (-GRID_W, 2 * q, 3 * q), (GRID_W, 3 * q, d_shift)]
    else:
        half = d_shift // 2
        bounds = [(-1, 0, half), (1, half, d_shift)]
    out = []
    for off, lo, hi in bounds:
        if lo < seg_start + seg_width and hi > seg_start:
            out.append((off, lo, hi))
    return out


def _prep_kernel(r_c, r_p, r_n, k_c, k_p, k_n, v_c, v_p, v_n, l_c, l_p, l_n,
                 mu_r, mu_k, mu_v, mu_l, w0_ref, a0_ref, w2w_ref, w2a_ref,
                 kk_ref, ka_ref, rk_ref, bd_ref,
                 r_o, v_o, kap_o, bon_o, w_o, kt_o, be_o,
                 rbuf, kbuf, vbuf, lbuf, *, tt, dr, d_shift, grid_mode, halo):
    i = pl.program_id(1)
    nt = pl.num_programs(1)
    first = i == 0
    last = i == nt - 1

    def fill(buf, cur, prv, nxt):
        buf[0:halo, :] = jnp.where(first, 0.0, prv[...])
        buf[halo:halo + tt, :] = cur[...]
        buf[halo + tt:halo + tt + halo, :] = jnp.where(last, 0.0, nxt[...])

    fill(rbuf, r_c, r_p, r_n)
    fill(kbuf, k_c, k_p, k_n)
    fill(vbuf, v_c, v_p, v_n)
    fill(lbuf, l_c, l_p, l_n)

    rowcol = lax.broadcasted_iota(jnp.int32, (tt, 1), 0) % GRID_W

    def shifted(buf, mu_ref, seg_start, c0, cw):
        u = buf[halo:halo + tt, c0:c0 + cw]
        ch = seg_start + c0 + lax.broadcasted_iota(jnp.int32, (1, cw), 1)
        s = jnp.zeros_like(u)
        for off, lo, hi in _shift_sources(seg_start + c0, cw, d_shift, grid_mode):
            nb = buf[halo + off:halo + off + tt, c0:c0 + cw]
            ok = (ch >= lo) & (ch < hi)
            if grid_mode and off == -1:
                ok = ok & (rowcol != 0)
            if grid_mode and off == 1:
                ok = ok & (rowcol != GRID_W - 1)
            s = jnp.where(ok, nb, s)
        return u + mu_ref[:, c0:c0 + cw] * (s - u)

    lo_ = shifted(lbuf, mu_l, 3 * dr, 0, LORA_PAD)
    tw = jnp.tanh(lo_).astype(BF16)
    lo_b = lo_.astype(BF16)
    bd = bd_ref[...].astype(F32)

    for g in range(dr // GROUP_W):
        c0 = g * GROUP_W
        sl = slice(c0, c0 + GROUP_W)
        r = shifted(rbuf, mu_r, 0, c0, GROUP_W)
        k = shifted(kbuf, mu_k, dr, c0, GROUP_W)
        v = shifted(vbuf, mu_v, 2 * dr, c0, GROUP_W)
        kk = k * kk_ref[:, sl]
        ss = jnp.dot(kk * kk, bd, preferred_element_type=F32, precision=lax.Precision.HIGHEST)
        kap = kk * lax.rsqrt(jnp.maximum(ss, 1e-24))
        r_o[:, sl] = r
        v_o[:, sl] = v
        kap_o[:, sl] = kap
        ktsum = jnp.zeros_like(k)
        for d in range(2):
            wl = w0_ref[d:d + 1, sl] + jnp.dot(tw, w2w_ref[d, :, sl], preferred_element_type=F32)
            w_o[d, :, sl] = jnp.exp(-(EXP_M05 * jax.nn.sigmoid(wl)))
            a = jax.nn.sigmoid(a0_ref[d:d + 1, sl]
                               + jnp.dot(lo_b, w2a_ref[d, :, sl], preferred_element_type=F32))
            kt = k * (1.0 + (a - 1.0) * ka_ref[:, sl])
            kt_o[d, :, sl] = kt
            be_o[d, :, sl] = kap * a
            ktsum = ktsum + kt
        cs = jnp.dot(r * ktsum * rk_ref[:, sl], bd, preferred_element_type=F32,
                     precision=lax.Precision.HIGHEST)
        bon_o[:, sl] = cs * v


def _prep(proj, lw, grid_mode):
    b, t, dp = proj.shape
    dr = lw["dr"]
    d_shift = lw["d_shift"]
    halo = GRID_W if grid_mode else 8
    tt = min(128, t)
    assert tt % halo == 0 and t % tt == 0
    nh = tt // halo
    n_halo_blocks = t // halo
    lora_blk = (7 * dr) // LORA_PAD

    def seg_specs(col_blk, width):
        return [
            pl.BlockSpec((None, tt, width), lambda i, j: (i, j, col_blk)),
            pl.BlockSpec((None, halo, width),
                         lambda i, j: (i, jnp.maximum(j * nh - 1, 0), col_blk)),
            pl.BlockSpec((None, halo, width),
                         lambda i, j: (i, jnp.minimum((j + 1) * nh, n_halo_blocks - 1), col_blk)),
        ]

    def const(shape):
        nd = len(shape)
        return pl.BlockSpec(shape, lambda i, j: (0,) * nd)

    in_specs = (seg_specs(0, dr) + seg_specs(1, dr) + seg_specs(2, dr) + seg_specs(lora_blk, LORA_PAD)
                + [const((1, dr)), const((1, dr)), const((1, dr)), const((1, LORA_PAD)),
                   const((2, dr)), const((2, dr)), const((2, LORA_PAD, dr)), const((2, LORA_PAD, dr)),
                   const((1, dr)), const((1, dr)), const((1, dr)), const((GROUP_W, GROUP_W))])
    tok = pl.BlockSpec((None, tt, dr), lambda i, j: (i, j, 0))
    tok2 = pl.BlockSpec((2, None, tt, dr), lambda i, j: (0, i, j, 0))
    sd = jax.ShapeDtypeStruct((b, t, dr), F32)
    sd2 = jax.ShapeDtypeStruct((2, b, t, dr), F32)
    args = [proj] * 12 + [lw["mu_r"], lw["mu_k"], lw["mu_v"], lw["mu_l"], lw["w0"], lw["a0"],
                          lw["w2w"], lw["w2a"], lw["k_k"], lw["k_a"], lw["r_k"], lw["bd"]]
    return pl.pallas_call(
        functools.partial(_prep_kernel, tt=tt, dr=dr, d_shift=d_shift, grid_mode=grid_mode, halo=halo),
        grid=(b, t // tt),
        in_specs=in_specs,
        out_specs=[tok, tok, tok, tok, tok2, tok2, tok2],
        out_shape=[sd, sd, sd, sd, sd2, sd2, sd2],
        scratch_shapes=[pltpu.VMEM((tt + 2 * halo, dr), F32)] * 3
                       + [pltpu.VMEM((tt + 2 * halo, LORA_PAD), F32)],
        compiler_params=_cparams(("parallel", "parallel")),
        name="prep",
    )(*args)


def _scan_kernel(kap_ref, r_ref, v_ref, w_ref, kt_ref, be_ref, s0_ref, bd_ref, e_ref,
                 o_ref, sf_ref, st, oacc, *, tc, ng):
    d = pl.program_id(1)
    c = pl.program_id(2)

    @pl.when(c == 0)
    def _():
        st[...] = s0_ref[...]

    oacc[...] = jnp.zeros_like(oacc)
    bd = bd_ref[...]
    diag = (lax.broadcasted_iota(jnp.int32, (HEAD_DIM, GROUP_W), 1) % HEAD_DIM
            == lax.broadcasted_iota(jnp.int32, (HEAD_DIM, GROUP_W), 0))

    for blk in range(tc // OUT_BLK):
        def step(jj, carry):
            i = blk * OUT_BLK + jj
            t = i + d * (tc - 1 - 2 * i)
            ecol = jj + d * (OUT_BLK - 1 - 2 * jj)
            e = e_ref[ecol]
            for g in range(ng):
                sl = slice(g * GROUP_W, (g + 1) * GROUP_W)
                s = st[g]
                kap = kap_ref[pl.ds(t, 1), sl]
                sa = _bdot(s * kap, bd)
                vbc = _bdot(jnp.where(diag, v_ref[pl.ds(t, 1), sl], 0.0), bd)
                s = (s * w_ref[pl.ds(t, 1), sl] - sa * be_ref[pl.ds(t, 1), sl]
                     + vbc * kt_ref[pl.ds(t, 1), sl])
                st[g] = s
                oacc[g] += _bdot(s * r_ref[pl.ds(t, 1), sl], e)
            return carry

        lax.fori_loop(0, OUT_BLK, step, 0)
        t0 = pl.multiple_of(blk * OUT_BLK + d * (tc - OUT_BLK - 2 * blk * OUT_BLK), OUT_BLK)
        for g in range(ng):
            ot = oacc[g].T
            for h in range(GROUP_HEADS):
                o_ref[g * GROUP_HEADS + h, pl.ds(t0, OUT_BLK), :] = ot[h * OUT_BLK:(h + 1) * OUT_BLK, :]
        oacc[...] = jnp.zeros_like(oacc)

    @pl.when(c == pl.num_programs(2) - 1)
    def _():
        sf_ref[...] = st[...]


def _scan(kap, r, v, w, kt, be, s0, bd, e_all):
    b, t, dr = kap.shape
    ng = dr // GROUP_W
    h = dr // HEAD_DIM
    tc = min(256, t)
    assert t % tc == 0 and tc % OUT_BLK == 0
    nt = t // tc

    def cidx(dd, cc):
        return cc + dd * (nt - 1 - 2 * cc)

    tok = pl.BlockSpec((None, tc, dr), lambda i, dd, cc: (i, cidx(dd, cc), 0))
    tok2 = pl.BlockSpec((None, None, tc, dr), lambda i, dd, cc: (dd, i, cidx(dd, cc), 0))
    return pl.pallas_call(
        functools.partial(_scan_kernel, tc=tc, ng=ng),
        grid=(b, 2, nt),
        in_specs=[tok, tok, tok, tok2, tok2, tok2,
                  pl.BlockSpec((None, None, ng, HEAD_DIM, GROUP_W), lambda i, dd, cc: (i, dd, 0, 0, 0)),
                  pl.BlockSpec((GROUP_W, GROUP_W), lambda i, dd, cc: (0, 0)),
                  pl.BlockSpec((OUT_BLK, GROUP_W, 128), lambda i, dd, cc: (0, 0, 0))],
        out_specs=[
            pl.BlockSpec((None, None, h, tc, HEAD_DIM), lambda i, dd, cc: (dd, i, 0, cidx(dd, cc), 0)),
            pl.BlockSpec((None, None, ng, HEAD_DIM, GROUP_W), lambda i, dd, cc: (i, dd, 0, 0, 0)),
        ],
        out_shape=[jax.ShapeDtypeStruct((2, b, h, t, HEAD_DIM), F32),
                   jax.ShapeDtypeStruct((b, 2, ng, HEAD_DIM, GROUP_W), F32)],
        scratch_shapes=[pltpu.VMEM((ng, HEAD_DIM, GROUP_W), F32),
                        pltpu.VMEM((ng, HEAD_DIM, 128), F32)],
        compiler_params=_cparams(("parallel", "parallel", "arbitrary")),
        name="rwkv7_scan",
    )(kap, r, v, w, kt, be, s0, bd, e_all)


def _post_kernel(o_ref, bon_ref, g_ref, p_c, p_p, p_n, q_c, q_p, q_n, gc_ref,
                 lng_ref, lnb_ref, cw_ref, cb_ref, clg_ref, clb_ref,
                 x_o, zbuf, cbuf, *, tt, dr, kc, halo):
    i = pl.program_id(1)
    first = i == 0
    last = i == pl.num_programs(1) - 1
    pad = kc // 2

    def normed(h):
        o = o_ref[0, h] + o_ref[1, h]
        mu = jnp.mean(o, axis=-1, keepdims=True)
        var = jnp.mean(jnp.square(o - mu), axis=-1, keepdims=True)
        return (o - mu) * lax.rsqrt(var + GN_EPS)

    for hp in range(dr // 128):
        sl = slice(hp * 128, (hp + 1) * 128)
        on = jnp.concatenate([normed(2 * hp), normed(2 * hp + 1)], axis=-1)
        xr = (on * lng_ref[:, sl] + lnb_ref[:, sl] + bon_ref[:, sl]) * _silu(g_ref[:, sl])
        x_o[:, sl] = xr.astype(x_o.dtype)

    def glu(p, q):
        return p * jax.nn.sigmoid(q)

    zbuf[0:halo, :] = jnp.where(first, 0.0, glu(p_p[...], q_p[...]))
    zbuf[halo:halo + tt, :] = glu(p_c[...], q_c[...])
    zbuf[halo + tt:halo + tt + halo, :] = jnp.where(last, 0.0, glu(p_n[...], q_n[...]))

    for cg in range(dr // 128):
        sl = slice(cg * 128, (cg + 1) * 128)
        acc = jnp.zeros((tt, 128), F32)
        for j in range(kc):
            start = halo - pad + j
            acc = acc + zbuf[start:start + tt, sl] * cw_ref[j:j + 1, sl]
        cbuf[:, sl] = acc + cb_ref[:, sl]

    z = cbuf[...]
    zm = jnp.mean(z, axis=-1, keepdims=True)
    zv = jnp.mean(jnp.square(z - zm), axis=-1, keepdims=True)
    zn = (z - zm) * lax.rsqrt(zv + LN_EPS) * clg_ref[...] + clb_ref[...]
    x_o[:, dr:] = (_silu(zn) * _silu(gc_ref[...])).astype(x_o.dtype)


def _post(o, bonus, proj, lw):
    _, b, h, t, _ = o.shape
    dr = lw["dr"]
    kc = lw["conv_w"].shape[0]
    halo = 16
    assert kc // 2 <= halo
    tt = min(128, t)
    nh = tt // halo
    n_halo_blocks = t // halo

    def seg(col_blk):
        return pl.BlockSpec((None, tt, dr), lambda i, j: (i, j, col_blk))

    def seg_halo(col_blk):
        return [
            seg(col_blk),
            pl.BlockSpec((None, halo, dr), lambda i, j: (i, jnp.maximum(j * nh - 1, 0), col_blk)),
            pl.BlockSpec((None, halo, dr),
                         lambda i, j: (i, jnp.minimum((j + 1) * nh, n_halo_blocks - 1), col_blk)),
        ]

    def const(shape):
        nd = len(shape)
        return pl.BlockSpec(shape, lambda i, j: (0,) * nd)

    in_specs = ([pl.BlockSpec((2, None, h, tt, HEAD_DIM), lambda i, j: (0, i, 0, j, 0)),
                 pl.BlockSpec((None, tt, dr), lambda i, j: (i, j, 0)),
                 seg(3)] + seg_halo(4) + seg_halo(5) + [seg(6)]
                + [const((1, dr)), const((1, dr)), const((kc, dr)), const((1, dr)),
                   const((1, dr)), const((1, dr))])
    args = [o, bonus] + [proj] * 8 + [lw["lnx_g"], lw["lnx_b"], lw["conv_w"], lw["conv_b"],
                                      lw["cln_g"], lw["cln_b"]]
    return pl.pallas_call(
        functools.partial(_post_kernel, tt=tt, dr=dr, kc=kc, halo=halo),
        grid=(b, t // tt),
        in_specs=in_specs,
        out_specs=pl.BlockSpec((None, tt, 2 * dr), lambda i, j: (i, j, 0)),
        out_shape=jax.ShapeDtypeStruct((b, t, 2 * dr), BF16),
        scratch_shapes=[pltpu.VMEM((tt + 2 * halo, dr), F32), pltpu.VMEM((tt, dr), F32)],
        compiler_params=_cparams(("parallel", "parallel")),
        name="post",
    )(*args)


def _block_diag_ones():
    idx = np.arange(GROUP_W) // HEAD_DIM
    return jnp.asarray((idx[:, None] == idx[None, :]).astype(np.float32), dtype=BF16)


def _column_selectors():
    e = np.zeros((OUT_BLK, GROUP_W, 128), np.float32)
    l = np.arange(GROUP_W)
    for j in range(OUT_BLK):
        e[j, l, (l // HEAD_DIM) * OUT_BLK + j] = 1.0
    return jnp.asarray(e, dtype=BF16)


def _layer_weights(l, w_in, mu_shift, decay_w0, decay_w2, iclr_a0, iclr_a2, k_k, k_a, r_k,
                   lnx_g, lnx_b, conv_w, conv_b, cln_g, cln_b, w_out, bd):
    d = w_in.shape[1]
    dr = k_k.shape[1]
    lw_ = decay_w2.shape[2]
    la_ = iclr_a2.shape[2]
    assert lw_ + la_ <= LORA_PAD and conv_w.shape[2] == dr and d == 2 * dr
    d_shift = 3 * dr + lw_ + la_
    wl = w_in[l]
    lora_cols = jnp.pad(wl[:, 3 * dr:d_shift], ((0, 0), (0, LORA_PAD - lw_ - la_)))
    w_perm = jnp.concatenate([wl[:, :3 * dr], wl[:, d_shift:], lora_cols], axis=1).astype(BF16)
    mu = mu_shift[l]
    w2w = jnp.zeros((2, LORA_PAD, dr), F32).at[:, :lw_].set(decay_w2[l]).astype(BF16)
    w2a = jnp.zeros((2, LORA_PAD, dr), F32).at[:, lw_:lw_ + la_].set(iclr_a2[l]).astype(BF16)
    row = lambda x: x.reshape(1, -1)
    return dict(
        dr=dr, d_shift=d_shift, w_in=w_perm, w_out=w_out[l].astype(BF16),
        mu_r=row(mu[:dr]), mu_k=row(mu[dr:2 * dr]), mu_v=row(mu[2 * dr:3 * dr]),
        mu_l=row(jnp.pad(mu[3 * dr:], (0, LORA_PAD - lw_ - la_))),
        w0=decay_w0[l], a0=iclr_a0[l], w2w=w2w, w2a=w2a,
        k_k=row(k_k[l]), k_a=row(k_a[l]), r_k=row(r_k[l]),
        lnx_g=row(lnx_g[l]), lnx_b=row(lnx_b[l]), conv_w=conv_w[l], conv_b=row(conv_b[l]),
        cln_g=row(cln_g[l]), cln_b=row(cln_b[l]), bd=bd)


def _to_tiles(s):
    b, two, h, n, _ = s.shape
    s = s.reshape(b, two, h // GROUP_HEADS, GROUP_HEADS, n, n)
    return jnp.transpose(s, (0, 1, 2, 4, 3, 5)).reshape(b, two, h // GROUP_HEADS, n, GROUP_W)


def _from_tiles(s):
    b, two, ng, n, _ = s.shape
    s = s.reshape(b, two, ng, n, GROUP_HEADS, n)
    return jnp.transpose(s, (0, 1, 2, 4, 3, 5)).reshape(b, two, ng * GROUP_HEADS, n, n)


def _mixer_layer(x, mod, s0_tiles, lw, norm_g, grid_mode, per_batch_row, e_all):
    b, t, d = x.shape
    h = _norm_mod(x, norm_g, mod, per_batch_row)
    proj = _in_proj(h.reshape(b * t, d), lw["w_in"]).reshape(b, t, -1)
    r, v, kap, bonus, w, kt, be = _prep(proj, lw, grid_mode)
    o, s_fin = _scan(kap, r, v, w, kt, be, s0_tiles, lw["bd"], e_all)
    xcat = _post(o, bonus, proj, lw)
    x_new = _out_proj(xcat.reshape(b * t, d), lw["w_out"], x.reshape(b * t, d), mod[:, 2 * d:],
                      t if per_batch_row else None).reshape(b, t, d)
    return x_new, s_fin


def kernel(x_prompt, x_sample, state_rwkv, c, c_ctx, norm_g, ada_w, ada_b, w_in, mu_shift,
           decay_w0, decay_w2, iclr_a0, iclr_a2, k_k, k_a, r_k, lnx_g, lnx_b, conv_w, conv_b,
           cln_g, cln_b, w_out, final_g):
    depth = w_in.shape[0]
    d = x_prompt.shape[-1]
    nb_ctx = x_prompt.shape[0]
    nb_lat = x_sample.shape[0]
    assert nb_lat + 1 <= 8
    dr = k_k.shape[1]
    hh = dr // HEAD_DIM

    cond8 = jnp.zeros((8, d), F32).at[0].set(c_ctx).at[1:1 + nb_lat].set(c)
    mod = _modulation(cond8, ada_w, ada_b)

    bd = _block_diag_ones()
    e_all = _column_selectors()
    zeros_state = jnp.zeros((nb_ctx, 2, hh // GROUP_HEADS, HEAD_DIM, GROUP_W), F32)
    r_k_flat = r_k.reshape(depth, dr)

    x_ctx, x_lat = x_prompt, x_sample
    new_states = []
    for l in range(depth):
        lw = _layer_weights(l, w_in, mu_shift, decay_w0, decay_w2, iclr_a0, iclr_a2, k_k, k_a,
                            r_k_flat, lnx_g, lnx_b, conv_w, conv_b, cln_g, cln_b, w_out, bd)
        x_ctx, s_fin = _mixer_layer(x_ctx, mod[l], zeros_state, lw, norm_g[l], False, False, e_all)
        new_states.append(_from_tiles(s_fin))
        x_lat, _ = _mixer_layer(x_lat, mod[l], _to_tiles(state_rwkv[:, l]), lw, norm_g[l],
                                True, True, e_all)
    y_prompt = _final_norm(x_ctx, final_g)
    y_sample = _final_norm(x_lat, final_g)
    return (y_prompt, y_sample, jnp.stack(new_states, axis=1))
```

```python
import functools

import numpy as np
import jax
import jax.numpy as jnp
from jax import lax
from jax.experimental import pallas as pl
from jax.experimental.pallas import tpu as pltpu

F32 = jnp.float32
BF16 = jnp.bfloat16

HEAD_DIM = 64
GRID_W = 64
LORA_PAD = 256
GROUP_HEADS = 4
GROUP_W = GROUP_HEADS * HEAD_DIM
OUT_BLK = 32
SCAN_SEQS = 4
VX_STEPS = 64
NORM_EPS = 1e-6
GN_EPS = 64e-5
LN_EPS = 1e-5
EXP_M05 = float(np.exp(-0.5))
VMEM_LIMIT_BYTES = 56 * 1024 * 1024


def _cparams(sem):
    return pltpu.CompilerParams(dimension_semantics=sem, vmem_limit_bytes=VMEM_LIMIT_BYTES)


def _silu(x):
    return x * jax.nn.sigmoid(x)


def _bdot(a, b):
    return jnp.dot(a.astype(BF16), b.astype(BF16), preferred_element_type=F32)


def _mod_kernel(c_ref, w_ref, b_ref, o_ref):
    o_ref[...] = _bdot(_silu(c_ref[...]), w_ref[...]) + b_ref[...]


def _modulation(cond8, ada_w, ada_b):
    depth, d, n3 = ada_w.shape
    tn = min(512, n3)
    return pl.pallas_call(
        _mod_kernel,
        grid=(depth, n3 // tn),
        in_specs=[
            pl.BlockSpec((8, d), lambda l, j: (0, 0)),
            pl.BlockSpec((None, d, tn), lambda l, j: (l, 0, j)),
            pl.BlockSpec((None, 1, tn), lambda l, j: (l, 0, j)),
        ],
        out_specs=pl.BlockSpec((None, 8, tn), lambda l, j: (l, 0, j)),
        out_shape=jax.ShapeDtypeStruct((depth, 8, n3), F32),
        compiler_params=_cparams(("parallel", "parallel")),
        name="modulation",
    )(cond8, ada_w, ada_b.reshape(depth, 1, n3))


def _norm_mod_kernel(x_ref, g_ref, mod_ref, h_ref, *, d, per_batch_row):
    row = (1 + pl.program_id(0)) if per_batch_row else 0
    x = x_ref[...]
    xn = x * lax.rsqrt(jnp.mean(x * x, axis=-1, keepdims=True) + NORM_EPS) * g_ref[...]
    m = mod_ref[pl.ds(row, 1), :]
    h_ref[...] = (xn * (1.0 + m[:, d:2 * d]) + m[:, :d]).astype(h_ref.dtype)


def _norm_mod(x, g, mod, per_batch_row):
    b, t, d = x.shape
    tt = min(256, t)
    return pl.pallas_call(
        functools.partial(_norm_mod_kernel, d=d, per_batch_row=per_batch_row),
        grid=(b, t // tt),
        in_specs=[
            pl.BlockSpec((None, tt, d), lambda i, j: (i, j, 0)),
            pl.BlockSpec((1, d), lambda i, j: (0, 0)),
            pl.BlockSpec((8, 3 * d), lambda i, j: (0, 0)),
        ],
        out_specs=pl.BlockSpec((None, tt, d), lambda i, j: (i, j, 0)),
        out_shape=jax.ShapeDtypeStruct((b, t, d), BF16),
        compiler_params=_cparams(("parallel", "parallel")),
        name="norm_mod",
    )(x, g.reshape(1, d), mod)


def _final_norm_kernel(x_ref, g_ref, o_ref):
    x = x_ref[...]
    o_ref[...] = x * lax.rsqrt(jnp.mean(x * x, axis=-1, keepdims=True) + NORM_EPS) * g_ref[...]


def _final_norm(x, g):
    b, t, d = x.shape
    tt = min(256, t)
    return pl.pallas_call(
        _final_norm_kernel,
        grid=(b, t // tt),
        in_specs=[
            pl.BlockSpec((None, tt, d), lambda i, j: (i, j, 0)),
            pl.BlockSpec((1, d), lambda i, j: (0, 0)),
        ],
        out_specs=pl.BlockSpec((None, tt, d), lambda i, j: (i, j, 0)),
        out_shape=jax.ShapeDtypeStruct((b, t, d), F32),
        compiler_params=_cparams(("parallel", "parallel")),
        name="final_norm",
    )(x, g.reshape(1, d))


def _mm_kernel(a_ref, w_ref, o_ref):
    o_ref[...] = jnp.dot(a_ref[...], w_ref[...], preferred_element_type=F32)


def _pick_tile(n, unit, cap):
    best = unit
    k = 1
    while k * unit <= min(n, cap):
        if n % (k * unit) == 0:
            best = k * unit
        k += 1
    return best


def _in_proj(h, w):
    m, k = h.shape
    n = w.shape[1]
    tm = _pick_tile(m, 256, 1024)
    tn = _pick_tile(n, 256, 768)
    return pl.pallas_call(
        _mm_kernel,
        grid=(m // tm, n // tn),
        in_specs=[
            pl.BlockSpec((tm, k), lambda i, j: (i, 0)),
            pl.BlockSpec((k, tn), lambda i, j: (0, j)),
        ],
        out_specs=pl.BlockSpec((tm, tn), lambda i, j: (i, j)),
        out_shape=jax.ShapeDtypeStruct((m, n), F32),
        compiler_params=_cparams(("parallel", "arbitrary")),
        name="in_proj",
    )(h, w)


def _out_proj_kernel(a_ref, w_ref, x_ref, gt_ref, o_ref, *, rows_per_batch, tm):
    if rows_per_batch is None:
        row = 0
    else:
        row = 1 + (pl.program_id(0) * tm) // rows_per_batch
    y = jnp.dot(a_ref[...], w_ref[...], preferred_element_type=F32)
    o_ref[...] = x_ref[...] + gt_ref[pl.ds(row, 1), :] * y


def _out_proj(a, w, x, gt8, rows_per_batch):
    m, k = a.shape
    n = w.shape[1]
    tm = _pick_tile(m, 256, 512)
    if rows_per_batch is not None:
        assert rows_per_batch % tm == 0
    tn = _pick_tile(n, 256, 1024)
    return pl.pallas_call(
        functools.partial(_out_proj_kernel, rows_per_batch=rows_per_batch, tm=tm),
        grid=(m // tm, n // tn),
        in_specs=[
            pl.BlockSpec((tm, k), lambda i, j: (i, 0)),
            pl.BlockSpec((k, tn), lambda i, j: (0, j)),
            pl.BlockSpec((tm, tn), lambda i, j: (i, j)),
            pl.BlockSpec((8, tn), lambda i, j: (0, j)),
        ],
        out_specs=pl.BlockSpec((tm, tn), lambda i, j: (i, j)),
        out_shape=jax.ShapeDtypeStruct((m, n), F32),
        compiler_params=_cparams(("parallel", "arbitrary")),
        name="out_proj",
    )(a, w, x, gt8)


def _shift_sources(seg_start, seg_width, d_shift, grid_mode):
    if grid_mode:
        q = d_shift // 4
        bounds = [(-1, 0, q), (1, q, 2 * q), (-GRID_W, 2 * q, 3 * q), (GRID_W, 3 * q, d_shift)]
    else:
        half = d_shift // 2
        bounds = [(-1, 0, half), (1, half, d_shift)]
    out = []
    for off, lo, hi in bounds:
        if lo < seg_start + seg_width and hi > seg_start:
            out.append((off, lo, hi))
    return out


def _prep_kernel(r_c, r_p, r_n, k_c, k_p, k_n, v_c, v_p, v_n, l_c, l_p, l_n,
                 mu_r, mu_k, mu_v, mu_l, w0_ref, a0_ref, w2w_ref, w2a_ref,
                 kk_ref, ka_ref, rk_ref, bd_ref,
                 r_o, vx_o, kap_o, bon_o, w_o, kt_o, be_o,
                 rbuf, kbuf, vbuf, lbuf, *, tt, dr, d_shift, grid_mode, halo):
    i = pl.program_id(1)
    nt = pl.num_programs(1)
    first = i == 0
    last = i == nt - 1

    def fill(buf, cur, prv, nxt):
        buf[0:halo, :] = jnp.where(first, 0.0, prv[...])
        buf[halo:halo + tt, :] = cur[...]
        buf[halo + tt:halo + tt + halo, :] = jnp.where(last, 0.0, nxt[...])

    fill(rbuf, r_c, r_p, r_n)
    fill(kbuf, k_c, k_p, k_n)
    fill(vbuf, v_c, v_p, v_n)
    fill(lbuf, l_c, l_p, l_n)

    rowcol = lax.broadcasted_iota(jnp.int32, (tt, 1), 0) % GRID_W

    def shifted(buf, mu_ref, seg_start, c0, cw):
        u = buf[halo:halo + tt, c0:c0 + cw]
        ch = seg_start + c0 + lax.broadcasted_iota(jnp.int32, (1, cw), 1)
        s = jnp.zeros_like(u)
        for off, lo, hi in _shift_sources(seg_start + c0, cw, d_shift, grid_mode):
            nb = buf[halo + off:halo + off + tt, c0:c0 + cw]
            ok = (ch >= lo) & (ch < hi)
            if grid_mode and off == -1:
                ok = ok & (rowcol != 0)
            if grid_mode and off == 1:
                ok = ok & (rowcol != GRID_W - 1)
            s = jnp.where(ok, nb, s)
        return u + mu_ref[:, c0:c0 + cw] * (s - u)

    lo_ = shifted(lbuf, mu_l, 3 * dr, 0, LORA_PAD)
    tw = jnp.tanh(lo_).astype(BF16)
    lo_b = lo_.astype(BF16)
    bd = bd_ref[...].astype(F32)

    lane128 = lax.broadcasted_iota(jnp.int32, (HEAD_DIM, 128), 1)

    for g in range(dr // GROUP_W):
        c0 = g * GROUP_W
        sl = slice(c0, c0 + GROUP_W)
        r = shifted(rbuf, mu_r, 0, c0, GROUP_W)
        k = shifted(kbuf, mu_k, dr, c0, GROUP_W)
        v = shifted(vbuf, mu_v, 2 * dr, c0, GROUP_W)
        kk = k * kk_ref[:, sl]
        ss = jnp.dot(kk * kk, bd, preferred_element_type=F32, precision=lax.Precision.HIGHEST)
        kap = kk * lax.rsqrt(jnp.maximum(ss, 1e-24))
        r_o[:, sl] = r
        kap_o[:, sl] = kap
        for q in range(GROUP_W // 128):
            vt = v[:, q * 128:(q + 1) * 128].T
            first, second = vt[:HEAD_DIM], vt[HEAD_DIM:]
            pair = g * (GROUP_W // 128) + q
            vx_o[0, pair] = jnp.where(lane128 < VX_STEPS, first, pltpu.roll(second, VX_STEPS, axis=1))
            vx_o[1, pair] = jnp.where(lane128 < VX_STEPS, pltpu.roll(first, VX_STEPS, axis=1), second)
        ktsum = jnp.zeros_like(k)
        for d in range(2):
            wl = w0_ref[d:d + 1, sl] + jnp.dot(tw, w2w_ref[d, :, sl], preferred_element_type=F32)
            w_o[d, :, sl] = jnp.exp(-(EXP_M05 * jax.nn.sigmoid(wl)))
            a = jax.nn.sigmoid(a0_ref[d:d + 1, sl]
                               + jnp.dot(lo_b, w2a_ref[d, :, sl], preferred_element_type=F32))
            kt = k * (1.0 + (a - 1.0) * ka_ref[:, sl])
            kt_o[d, :, sl] = kt
            be_o[d, :, sl] = kap * a
            ktsum = ktsum + kt
        cs = jnp.dot(r * ktsum * rk_ref[:, sl], bd, preferred_element_type=F32,
                     precision=lax.Precision.HIGHEST)
        bon_o[:, sl] = cs * v


def _prep(proj, lw, grid_mode):
    b, t, dp = proj.shape
    dr = lw["dr"]
    d_shift = lw["d_shift"]
    halo = GRID_W if grid_mode else 8
    tt = 2 * VX_STEPS
    assert tt % halo == 0 and t % tt == 0
    nh = tt // halo
    n_halo_blocks = t // halo
    lora_blk = (7 * dr) // LORA_PAD

    def seg_specs(col_blk, width):
        return [
            pl.BlockSpec((None, tt, width), lambda i, j: (i, j, col_blk)),
            pl.BlockSpec((None, halo, width),
                         lambda i, j: (i, jnp.maximum(j * nh - 1, 0), col_blk)),
            pl.BlockSpec((None, halo, width),
                         lambda i, j: (i, jnp.minimum((j + 1) * nh, n_halo_blocks - 1), col_blk)),
        ]

    def const(shape):
        nd = len(shape)
        return pl.BlockSpec(shape, lambda i, j: (0,) * nd)

    in_specs = (seg_specs(0, dr) + seg_specs(1, dr) + seg_specs(2, dr) + seg_specs(lora_blk, LORA_PAD)
                + [const((1, dr)), const((1, dr)), const((1, dr)), const((1, LORA_PAD)),
                   const((2, dr)), const((2, dr)), const((2, LORA_PAD, dr)), const((2, LORA_PAD, dr)),
                   const((1, dr)), const((1, dr)), const((1, dr)), const((GROUP_W, GROUP_W))])
    tok = pl.BlockSpec((None, tt, dr), lambda i, j: (i, j, 0))
    tok2 = pl.BlockSpec((2, None, tt, dr), lambda i, j: (0, i, j, 0))
    sd = jax.ShapeDtypeStruct((b, t, dr), F32)
    sd2 = jax.ShapeDtypeStruct((2, b, t, dr), F32)
    args = [proj] * 12 + [lw["mu_r"], lw["mu_k"], lw["mu_v"], lw["mu_l"], lw["w0"], lw["a0"],
                          lw["w2w"], lw["w2a"], lw["k_k"], lw["k_a"], lw["r_k"], lw["bd"]]
    return pl.pallas_call(
        functools.partial(_prep_kernel, tt=tt, dr=dr, d_shift=d_shift, grid_mode=grid_mode, halo=halo),
        grid=(b, t // tt),
        in_specs=in_specs,
        out_specs=[tok,
                   pl.BlockSpec((None, tt // VX_STEPS, dr // 128, HEAD_DIM, 128),
                                lambda i, j: (i, j, 0, 0, 0)),
                   tok, tok, tok2, tok2, tok2],
        out_shape=[sd, jax.ShapeDtypeStruct((b, t // VX_STEPS, dr // 128, HEAD_DIM, 128), F32),
                   sd, sd, sd2, sd2, sd2],
        scratch_shapes=[pltpu.VMEM((tt + 2 * halo, dr), F32)] * 3
                       + [pltpu.VMEM((tt + 2 * halo, LORA_PAD), F32)],
        compiler_params=_cparams(("parallel", "parallel")),
        name="prep",
    )(*args)


def _scan_kernel(kap_ref, r_ref, vx_ref, w_ref, kt_ref, be_ref, s0_ref, bd_ref, e_ref,
                 o_ref, sf_ref, st, lhs, qbuf, oacc, *, ng, nb):
    d = pl.program_id(1)
    c = pl.program_id(2)

    @pl.when(c == 0)
    def _():
        st[...] = s0_ref[...]
        qbuf[...] = jnp.zeros_like(qbuf)

    oacc[...] = jnp.zeros_like(oacc)
    bd = bd_ref[...]
    tiles =[(s, g, (s * ng + g) * HEAD_DIM, slice(g * GROUP_W, (g + 1) * GROUP_W))
             for s in range(nb) for g in range(ng)]

    def column(j):
        return j + d * (OUT_BLK - 1 - 2 * j)

    def reduce_outputs(ecol):
        oc = jnp.dot(qbuf[...], e_ref[ecol], preferred_element_type=F32)
        for s, g, r0, _ in tiles:
            oacc[s, g] += oc[r0:r0 + HEAD_DIM]

    chunk = c + d * (pl.num_programs(2) - 1 - 2 * c)
    half = (chunk % (VX_STEPS // OUT_BLK)) * OUT_BLK
    lane = lax.broadcasted_iota(jnp.int32, (HEAD_DIM, 128), 1)

    def step(i, carry):
        t = column(i)
        idx = jnp.where(lane < HEAD_DIM, half + t, VX_STEPS + half + t)
        for s, g, r0, sl in tiles:
            lhs[r0:r0 + HEAD_DIM, :] = (st[s, g] * kap_ref[s, pl.ds(t, 1), sl]).astype(BF16)
        res = jnp.dot(lhs[...], bd, preferred_element_type=F32)
        reduce_outputs(jnp.where(i == 0, OUT_BLK, column(i - 1)))
        for s, g, r0, sl in tiles:
            s_new = (st[s, g] * w_ref[s, pl.ds(t, 1), sl]
                     - res[r0:r0 + HEAD_DIM] * be_ref[s, pl.ds(t, 1), sl]
                     + jnp.concatenate(
                         [jnp.take_along_axis(vx_ref[s, 2 * g + q], idx, axis=1) for q in range(2)],
                         axis=1) * kt_ref[s, pl.ds(t, 1), sl])
            st[s, g] = s_new
            qbuf[r0:r0 + HEAD_DIM, :] = (s_new * r_ref[s, pl.ds(t, 1), sl]).astype(BF16)
        return carry

    lax.fori_loop(0, OUT_BLK, step, 0)
    reduce_outputs(column(OUT_BLK - 1))
    for s, g, _, _ in tiles:
        ot = oacc[s, g].T
        for h in range(GROUP_HEADS):
            o_ref[s, g * GROUP_HEADS + h] = ot[h * OUT_BLK:(h + 1) * OUT_BLK, :]

    @pl.when(c == pl.num_programs(2) - 1)
    def _():
        sf_ref[...] = st[...]


def _scan(kap, r, vx, w, kt, be, s0, bd, e_all):
    b, t, dr = kap.shape
    ng = dr // GROUP_W
    h = dr // HEAD_DIM
    tc = OUT_BLK
    assert t % tc == 0
    nt = t // tc
    nb = _pick_tile(b, 1, SCAN_SEQS)
    rows = nb * ng * HEAD_DIM

    def cidx(dd, cc):
        return cc + dd * (nt - 1 - 2 * cc)

    tok = pl.BlockSpec((nb, tc, dr), lambda i, dd, cc: (i, cidx(dd, cc), 0))
    tok2 = pl.BlockSpec((None, nb, tc, dr), lambda i, dd, cc: (dd, i, cidx(dd, cc), 0))
    state = pl.BlockSpec((nb, None, ng, HEAD_DIM, GROUP_W), lambda i, dd, cc: (i, dd, 0, 0, 0))
    vxs = pl.BlockSpec((nb, None, 2 * ng, HEAD_DIM, 128),
                       lambda i, dd, cc: (i, cidx(dd, cc) // (VX_STEPS // OUT_BLK), 0, 0, 0))
    return pl.pallas_call(
        functools.partial(_scan_kernel, ng=ng, nb=nb),
        grid=(b // nb, 2, nt),
        in_specs=[tok, tok, vxs, tok2, tok2, tok2, state,
                  pl.BlockSpec((GROUP_W, GROUP_W), lambda i, dd, cc: (0, 0)),
                  pl.BlockSpec((OUT_BLK + 1, GROUP_W, 128), lambda i, dd, cc: (0, 0, 0))],
        out_specs=[
            pl.BlockSpec((None, nb, h, tc, HEAD_DIM), lambda i, dd, cc: (dd, i, 0, cidx(dd, cc), 0)),
            state,
        ],
        out_shape=[jax.ShapeDtypeStruct((2, b, h, t, HEAD_DIM), F32),
                   jax.ShapeDtypeStruct((b, 2, ng, HEAD_DIM, GROUP_W), F32)],
        scratch_shapes=[pltpu.VMEM((nb, ng, HEAD_DIM, GROUP_W), F32),
                        pltpu.VMEM((rows, GROUP_W), BF16),
                        pltpu.VMEM((rows, GROUP_W), BF16),
                        pltpu.VMEM((nb, ng, HEAD_DIM, 128), F32)],
        compiler_params=_cparams(("parallel", "parallel", "arbitrary")),
        name="rwkv7_scan",
    )(kap, r, vx, w, kt, be, s0, bd, e_all)


def _post_kernel(o_ref, bon_ref, g_ref, p_c, p_p, p_n, q_c, q_p, q_n, gc_ref,
                 lng_ref, lnb_ref, cw_ref, cb_ref, clg_ref, clb_ref,
                 x_o, zbuf, cbuf, *, tt, dr, kc, halo):
    i = pl.program_id(1)
    first = i == 0
    last = i == pl.num_programs(1) - 1
    pad = kc // 2

    def normed(h):
        o = o_ref[0, h] + o_ref[1, h]
        mu = jnp.mean(o, axis=-1, keepdims=True)
        var = jnp.mean(jnp.square(o - mu), axis=-1, keepdims=True)
        return (o - mu) * lax.rsqrt(var + GN_EPS)

    for hp in range(dr // 128):
        sl = slice(hp * 128, (hp + 1) * 128)
        on = jnp.concatenate([normed(2 * hp), normed(2 * hp + 1)], axis=-1)
        xr = (on * lng_ref[:, sl] + lnb_ref[:, sl] + bon_ref[:, sl]) * _silu(g_ref[:, sl])
        x_o[:, sl] = xr.astype(x_o.dtype)

    def glu(p, q):
        return p * jax.nn.sigmoid(q)

    zbuf[0:halo, :] = jnp.where(first, 0.0, glu(p_p[...], q_p[...]))
    zbuf[halo:halo + tt, :] = glu(p_c[...], q_c[...])
    zbuf[halo + tt:halo + tt + halo, :] = jnp.where(last, 0.0, glu(p_n[...], q_n[...]))

    for cg in range(dr // 128):
        sl = slice(cg * 128, (cg + 1) * 128)
        acc = jnp.zeros((tt, 128), F32)
        for j in range(kc):
            start = halo - pad + j
            acc = acc + zbuf[start:start + tt, sl] * cw_ref[j:j + 1, sl]
        cbuf[:, sl] = acc + cb_ref[:, sl]

    z = cbuf[...]
    zm = jnp.mean(z, axis=-1, keepdims=True)
    zv = jnp.mean(jnp.square(z - zm), axis=-1, keepdims=True)
    zn = (z - zm) * lax.rsqrt(zv + LN_EPS) * clg_ref[...] + clb_ref[...]
    x_o[:, dr:] = (_silu(zn) * _silu(gc_ref[...])).astype(x_o.dtype)


def _post(o, bonus, proj, lw):
    _, b, h, t, _ = o.shape
    dr = lw["dr"]
    kc = lw["conv_w"].shape[0]
    halo = 16
    assert kc // 2 <= halo
    tt = min(128, t)
    nh = tt // halo
    n_halo_blocks = t // halo

    def seg(col_blk):
        return pl.BlockSpec((None, tt, dr), lambda i, j: (i, j, col_blk))

    def seg_halo(col_blk):
        return [
            seg(col_blk),
            pl.BlockSpec((None, halo, dr), lambda i, j: (i, jnp.maximum(j * nh - 1, 0), col_blk)),
            pl.BlockSpec((None, halo, dr),
                         lambda i, j: (i, jnp.minimum((j + 1) * nh, n_halo_blocks - 1), col_blk)),
        ]

    def const(shape):
        nd = len(shape)
        return pl.BlockSpec(shape, lambda i, j: (0,) * nd)

    in_specs = ([pl.BlockSpec((2, None, h, tt, HEAD_DIM), lambda i, j: (0, i, 0, j, 0)),
                 pl.BlockSpec((None, tt, dr), lambda i, j: (i, j, 0)),
                 seg(3)] + seg_halo(4) + seg_halo(5) + [seg(6)]
                + [const((1, dr)), const((1, dr)), const((kc, dr)), const((1, dr)),
                   const((1, dr)), const((1, dr))])
    args = [o, bonus] + [proj] * 8 + [lw["lnx_g"], lw["lnx_b"], lw["conv_w"], lw["conv_b"],
                                      lw["cln_g"], lw["cln_b"]]
    return pl.pallas_call(
        functools.partial(_post_kernel, tt=tt, dr=dr, kc=kc, halo=halo),
        grid=(b, t // tt),
        in_specs=in_specs,
        out_specs=pl.BlockSpec((None, tt, 2 * dr), lambda i, j: (i, j, 0)),
        out_shape=jax.ShapeDtypeStruct((b, t, 2 * dr), BF16),
        scratch_shapes=[pltpu.VMEM((tt + 2 * halo, dr), F32), pltpu.VMEM((tt, dr), F32)],
        compiler_params=_cparams(("parallel", "parallel")),
        name="post",
    )(*args)


def _block_diag_ones():
    idx = np.arange(GROUP_W) // HEAD_DIM
    return jnp.asarray((idx[:, None] == idx[None, :]).astype(np.float32), dtype=BF16)


def _column_selectors():
    e = np.zeros((OUT_BLK + 1, GROUP_W, 128), np.float32)
    l = np.arange(GROUP_W)
    for j in range(OUT_BLK):
        e[j, l, (l // HEAD_DIM) * OUT_BLK + j] = 1.0
    return jnp.asarray(e, dtype=BF16)


def _layer_weights(l, w_in, mu_shift, decay_w0, decay_w2, iclr_a0, iclr_a2, k_k, k_a, r_k,
                   lnx_g, lnx_b, conv_w, conv_b, cln_g, cln_b, w_out, bd):
    d = w_in.shape[1]
    dr = k_k.shape[1]
    lw_ = decay_w2.shape[2]
    la_ = iclr_a2.shape[2]
    assert lw_ + la_ <= LORA_PAD and conv_w.shape[2] == dr and d == 2 * dr
    d_shift = 3 * dr + lw_ + la_
    wl = w_in[l]
    lora_cols = jnp.pad(wl[:, 3 * dr:d_shift], ((0, 0), (0, LORA_PAD - lw_ - la_)))
    w_perm = jnp.concatenate([wl[:, :3 * dr], wl[:, d_shift:], lora_cols], axis=1).astype(BF16)
    mu = mu_shift[l]
    w2w = jnp.zeros((2, LORA_PAD, dr), F32).at[:, :lw_].set(decay_w2[l]).astype(BF16)
    w2a = jnp.zeros((2, LORA_PAD, dr), F32).at[:, lw_:lw_ + la_].set(iclr_a2[l]).astype(BF16)
    row = lambda x: x.reshape(1, -1)
    return dict(
        dr=dr, d_shift=d_shift, w_in=w_perm, w_out=w_out[l].astype(BF16),
        mu_r=row(mu[:dr]), mu_k=row(mu[dr:2 * dr]), mu_v=row(mu[2 * dr:3 * dr]),
        mu_l=row(jnp.pad(mu[3 * dr:], (0, LORA_PAD - lw_ - la_))),
        w0=decay_w0[l], a0=iclr_a0[l], w2w=w2w, w2a=w2a,
        k_k=row(k_k[l]), k_a=row(k_a[l]), r_k=row(r_k[l]),
        lnx_g=row(lnx_g[l]), lnx_b=row(lnx_b[l]), conv_w=conv_w[l], conv_b=row(conv_b[l]),
        cln_g=row(cln_g[l]), cln_b=row(cln_b[l]), bd=bd)


def _to_tiles(s):
    b, two, h, n, _ = s.shape
    s = s.reshape(b, two, h // GROUP_HEADS, GROUP_HEADS, n, n)
    return jnp.transpose(s, (0, 1, 2, 4, 3, 5)).reshape(b, two, h // GROUP_HEADS, n, GROUP_W)


def _from_tiles(s):
    b, two, ng, n, _ = s.shape
    s = s.reshape(b, two, ng, n, GROUP_HEADS, n)
    return jnp.transpose(s, (0, 1, 2, 4, 3, 5)).reshape(b, two, ng * GROUP_HEADS, n, n)


def _mixer_layer(x, mod, s0_tiles, lw, norm_g, grid_mode, per_batch_row, e_all):
    b, t, d = x.shape
    h = _norm_mod(x, norm_g, mod, per_batch_row)
    proj = _in_proj(h.reshape(b * t, d), lw["w_in"]).reshape(b, t, -1)
    r, vx, kap, bonus, w, kt, be = _prep(proj, lw, grid_mode)
    o, s_fin = _scan(kap, r, vx, w, kt, be, s0_tiles, lw["bd"], e_all)
    xcat = _post(o, bonus, proj, lw)
    x_new = _out_proj(xcat.reshape(b * t, d), lw["w_out"], x.reshape(b * t, d), mod[:, 2 * d:],
                      t if per_batch_row else None).reshape(b, t, d)
    return x_new, s_fin


def kernel(x_prompt, x_sample, state_rwkv, c, c_ctx, norm_g, ada_w, ada_b, w_in, mu_shift,
           decay_w0, decay_w2, iclr_a0, iclr_a2, k_k, k_a, r_k, lnx_g, lnx_b, conv_w, conv_b,
           cln_g, cln_b, w_out, final_g):
    depth = w_in.shape[0]
    d = x_prompt.shape[-1]
    nb_ctx = x_prompt.shape[0]
    nb_lat = x_sample.shape[0]
    assert nb_lat + 1 <= 8
    dr = k_k.shape[1]
    hh = dr // HEAD_DIM

    cond8 = jnp.zeros((8, d), F32).at[0].set(c_ctx).at[1:1 + nb_lat].set(c)
    mod = _modulation(cond8, ada_w, ada_b)

    bd = _block_diag_ones()
    e_all = _column_selectors()
    zeros_state = jnp.zeros((nb_ctx, 2, hh // GROUP_HEADS, HEAD_DIM, GROUP_W), F32)
    r_k_flat = r_k.reshape(depth, dr)

    x_ctx, x_lat = x_prompt, x_sample
    new_states = []
    for l in range(depth):
        lw = _layer_weights(l, w_in, mu_shift, decay_w0, decay_w2, iclr_a0, iclr_a2, k_k, k_a,
                            r_k_flat, lnx_g, lnx_b, conv_w, conv_b, cln_g, cln_b, w_out, bd)
        x_ctx, s_fin = _mixer_layer(x_ctx, mod[l], zeros_state, lw, norm_g[l], False, False, e_all)
        new_states.append(_from_tiles(s_fin))
        x_lat, _ = _mixer_layer(x_lat, mod[l], _to_tiles(state_rwkv[:, l]), lw, norm_g[l],
                                True, True, e_all)
    y_prompt = _final_norm(x_ctx, final_g)
    y_sample = _final_norm(x_lat, final_g)
    return (y_prompt, y_sample, jnp.stack(new_states, axis=1))
```

```python
import functools

import numpy as np
import jax
import jax.numpy as jnp
from jax import lax
from jax.experimental import pallas as pl
from jax.experimental.pallas import tpu as pltpu

F32 = jnp.float32
BF16 = jnp.bfloat16

HEAD_DIM = 64
GRID_W = 64
LORA_PAD = 256
GROUP_HEADS = 4
GROUP_W = GROUP_HEADS * HEAD_DIM
OUT_BLK = 32
SCAN_SEQS = 4
VX_STEPS = 64
NORM_EPS = 1e-6
GN_EPS = 64e-5
LN_EPS = 1e-5
EXP_M05 = float(np.exp(-0.5))
VMEM_LIMIT_BYTES = 56 * 1024 * 1024


def _cparams(sem):
    return pltpu.CompilerParams(dimension_semantics=sem, vmem_limit_bytes=VMEM_LIMIT_BYTES)


def _silu(x):
    return x * jax.nn.sigmoid(x)


def _bdot(a, b):
    return jnp.dot(a.astype(BF16), b.astype(BF16), preferred_element_type=F32)


def _mod_kernel(c_ref, w_ref, b_ref, o_ref):
    o_ref[...] = _bdot(_silu(c_ref[...]), w_ref[...]) + b_ref[...]


def _modulation(cond8, ada_w, ada_b):
    depth, d, n3 = ada_w.shape
    tn = min(512, n3)
    return pl.pallas_call(
        _mod_kernel,
        grid=(depth, n3 // tn),
        in_specs=[
            pl.BlockSpec((8, d), lambda l, j: (0, 0)),
            pl.BlockSpec((None, d, tn), lambda l, j: (l, 0, j)),
            pl.BlockSpec((None, 1, tn), lambda l, j: (l, 0, j)),
        ],
        out_specs=pl.BlockSpec((None, 8, tn), lambda l, j: (l, 0, j)),
        out_shape=jax.ShapeDtypeStruct((depth, 8, n3), F32),
        compiler_params=_cparams(("parallel", "parallel")),
        name="modulation",
    )(cond8, ada_w, ada_b.reshape(depth, 1, n3))


def _norm_mod_kernel(x_ref, g_ref, mod_ref, h_ref, *, d, per_batch_row):
    row = (1 + pl.program_id(0)) if per_batch_row else 0
    x = x_ref[...]
    xn = x * lax.rsqrt(jnp.mean(x * x, axis=-1, keepdims=True) + NORM_EPS) * g_ref[...]
    m = mod_ref[pl.ds(row, 1), :]
    h_ref[...] = (xn * (1.0 + m[:, d:2 * d]) + m[:, :d]).astype(h_ref.dtype)


def _norm_mod(x, g, mod, per_batch_row):
    b, t, d = x.shape
    tt = min(256, t)
    return pl.pallas_call(
        functools.partial(_norm_mod_kernel, d=d, per_batch_row=per_batch_row),
        grid=(b, t // tt),
        in_specs=[
            pl.BlockSpec((None, tt, d), lambda i, j: (i, j, 0)),
            pl.BlockSpec((1, d), lambda i, j: (0, 0)),
            pl.BlockSpec((8, 3 * d), lambda i, j: (0, 0)),
        ],
        out_specs=pl.BlockSpec((None, tt, d), lambda i, j: (i, j, 0)),
        out_shape=jax.ShapeDtypeStruct((b, t, d), BF16),
        compiler_params=_cparams(("parallel", "parallel")),
        name="norm_mod",
    )(x, g.reshape(1, d), mod)


def _final_norm_kernel(x_ref, g_ref, o_ref):
    x = x_ref[...]
    o_ref[...] = x * lax.rsqrt(jnp.mean(x * x, axis=-1, keepdims=True) + NORM_EPS) * g_ref[...]


def _final_norm(x, g):
    b, t, d = x.shape
    tt = min(256, t)
    return pl.pallas_call(
        _final_norm_kernel,
        grid=(b, t // tt),
        in_specs=[
            pl.BlockSpec((None, tt, d), lambda i, j: (i, j, 0)),
            pl.BlockSpec((1, d), lambda i, j: (0, 0)),
        ],
        out_specs=pl.BlockSpec((None, tt, d), lambda i, j: (i, j, 0)),
        out_shape=jax.ShapeDtypeStruct((b, t, d), F32),
        compiler_params=_cparams(("parallel", "parallel")),
        name="final_norm",
    )(x, g.reshape(1, d))


def _mm_kernel(a_ref, w_ref, o_ref):
    o_ref[...] = jnp.dot(a_ref[...], w_ref[...], preferred_element_type=F32)


def _pick_tile(n, unit, cap):
    best = unit
    k = 1
    while k * unit <= min(n, cap):
        if n % (k * unit) == 0:
            best = k * unit
        k += 1
    return best


def _in_proj(h, w):
    m, k = h.shape
    n = w.shape[1]
    tm = _pick_tile(m, 256, 1024)
    tn = _pick_tile(n, 256, 768)
    return pl.pallas_call(
        _mm_kernel,
        grid=(m // tm, n // tn),
        in_specs=[
            pl.BlockSpec((tm, k), lambda i, j: (i, 0)),
            pl.BlockSpec((k, tn), lambda i, j: (0, j)),
        ],
        out_specs=pl.BlockSpec((tm, tn), lambda i, j: (i, j)),
        out_shape=jax.ShapeDtypeStruct((m, n), F32),
        compiler_params=_cparams(("parallel", "arbitrary")),
        name="in_proj",
    )(h, w)


def _out_proj_kernel(a_ref, w_ref, x_ref, gt_ref, o_ref, *, rows_per_batch, tm):
    if rows_per_batch is None:
        row = 0
    else:
        row = 1 + (pl.program_id(0) * tm) // rows_per_batch
    y = jnp.dot(a_ref[...], w_ref[...], preferred_element_type=F32)
    o_ref[...] = x_ref[...] + gt_ref[pl.ds(row, 1), :] * y


def _out_proj(a, w, x, gt8, rows_per_batch):
    m, k = a.shape
    n = w.shape[1]
    tm = _pick_tile(m, 256, 512)
    if rows_per_batch is not None:
        assert rows_per_batch % tm == 0
    tn = _pick_tile(n, 256, 1024)
    return pl.pallas_call(
        functools.partial(_out_proj_kernel, rows_per_batch=rows_per_batch, tm=tm),
        grid=(m // tm, n // tn),
        in_specs=[
            pl.BlockSpec((tm, k), lambda i, j: (i, 0)),
            pl.BlockSpec((k, tn), lambda i, j: (0, j)),
            pl.BlockSpec((tm, tn), lambda i, j: (i, j)),
            pl.BlockSpec((8, tn), lambda i, j: (0, j)),
        ],
        out_specs=pl.BlockSpec((tm, tn), lambda i, j: (i, j)),
        out_shape=jax.ShapeDtypeStruct((m, n), F32),
        compiler_params=_cparams(("parallel", "arbitrary")),
        name="out_proj",
    )(a, w, x, gt8)


def _shift_sources(seg_start, seg_width, d_shift, grid_mode):
    if grid_mode:
        q = d_shift // 4
        bounds = [(-1, 0, q), (1, q, 2 * q), (-GRID_W, 2 * q, 3 * q), (GRID_W, 3 * q, d_shift)]
    else:
        half = d_shift // 2
        bounds = [(-1, 0, half), (1, half, d_shift)]
    out = []
    for off, lo, hi in bounds:
        if lo < seg_start + seg_width and hi > seg_start:
            out.append((off, lo, hi))
    return out


def _prep_kernel(r_c, r_p, r_n, k_c, k_p, k_n, v_c, v_p, v_n, l_c, l_p, l_n,
                 mu_r, mu_k, mu_v, mu_l, w0_ref, a0_ref, w2w_ref, w2a_ref,
                 kk_ref, ka_ref, rk_ref, bd_ref,
                 r_o, vx_o, kap_o, bon_o, w_o, kt_o, be_o,
                 rbuf, kbuf, vbuf, lbuf, *, tt, dr, d_shift, grid_mode, halo):
    i = pl.program_id(1)
    nt = pl.num_programs(1)
    first = i == 0
    last = i == nt - 1

    def fill(buf, cur, prv, nxt):
        buf[0:halo, :] = jnp.where(first, 0.0, prv[...])
        buf[halo:halo + tt, :] = cur[...]
        buf[halo + tt:halo + tt + halo, :] = jnp.where(last, 0.0, nxt[...])

    fill(rbuf, r_c, r_p, r_n)
    fill(kbuf, k_c, k_p, k_n)
    fill(vbuf, v_c, v_p, v_n)
    fill(lbuf, l_c, l_p, l_n)

    rowcol = lax.broadcasted_iota(jnp.int32, (tt, 1), 0) % GRID_W

    def shifted(buf, mu_ref, seg_start, c0, cw):
        u = buf[halo:halo + tt, c0:c0 + cw]
        ch = seg_start + c0 + lax.broadcasted_iota(jnp.int32, (1, cw), 1)
        s = jnp.zeros_like(u)
        for off, lo, hi in _shift_sources(seg_start + c0, cw, d_shift, grid_mode):
            nb = buf[halo + off:halo + off + tt, c0:c0 + cw]
            ok = (ch >= lo) & (ch < hi)
            if grid_mode and off == -1:
                ok = ok & (rowcol != 0)
            if grid_mode and off == 1:
                ok = ok & (rowcol != GRID_W - 1)
            s = jnp.where(ok, nb, s)
        return u + mu_ref[:, c0:c0 + cw] * (s - u)

    lo_ = shifted(lbuf, mu_l, 3 * dr, 0, LORA_PAD)
    tw = jnp.tanh(lo_).astype(BF16)
    lo_b = lo_.astype(BF16)
    bd = bd_ref[...].astype(F32)

    lane128 = lax.broadcasted_iota(jnp.int32, (HEAD_DIM, 128), 1)

    for g in range(dr // GROUP_W):
        c0 = g * GROUP_W
        sl = slice(c0, c0 + GROUP_W)
        r = shifted(rbuf, mu_r, 0, c0, GROUP_W)
        k = shifted(kbuf, mu_k, dr, c0, GROUP_W)
        v = shifted(vbuf, mu_v, 2 * dr, c0, GROUP_W)
        kk = k * kk_ref[:, sl]
        ss = jnp.dot(kk * kk, bd, preferred_element_type=F32, precision=lax.Precision.HIGHEST)
        kap = kk * lax.rsqrt(jnp.maximum(ss, 1e-24))
        r_o[:, sl] = r
        kap_o[:, sl] = kap
        for q in range(GROUP_W // 128):
            vt = v[:, q * 128:(q + 1) * 128].T
            first, second = vt[:HEAD_DIM], vt[HEAD_DIM:]
            pair = g * (GROUP_W // 128) + q
            vx_o[0, pair] = jnp.where(lane128 < VX_STEPS, first, pltpu.roll(second, VX_STEPS, axis=1))
            vx_o[1, pair] = jnp.where(lane128 < VX_STEPS, pltpu.roll(first, VX_STEPS, axis=1), second)
        ktsum = jnp.zeros_like(k)
        for d in range(2):
            wl = w0_ref[d:d + 1, sl] + jnp.dot(tw, w2w_ref[d, :, sl], preferred_element_type=F32)
            w_o[d, :, sl] = jnp.exp(-(EXP_M05 * jax.nn.sigmoid(wl)))
            a = jax.nn.sigmoid(a0_ref[d:d + 1, sl]
                               + jnp.dot(lo_b, w2a_ref[d, :, sl], preferred_element_type=F32))
            kt = k * (1.0 + (a - 1.0) * ka_ref[:, sl])
            kt_o[d, :, sl] = kt
            be_o[d, :, sl] = kap * a
            ktsum = ktsum + kt
        cs = jnp.dot(r * ktsum * rk_ref[:, sl], bd, preferred_element_type=F32,
                     precision=lax.Precision.HIGHEST)
        bon_o[:, sl] = cs * v


def _prep(proj, lw, grid_mode):
    b, t, dp = proj.shape
    dr = lw["dr"]
    d_shift = lw["d_shift"]
    halo = GRID_W if grid_mode else 8
    tt = 2 * VX_STEPS
    assert tt % halo == 0 and t % tt == 0
    nh = tt // halo
    n_halo_blocks = t // halo
    lora_blk = (7 * dr) // LORA_PAD

    def seg_specs(col_blk, width):
        return [
            pl.BlockSpec((None, tt, width), lambda i, j: (i, j, col_blk)),
            pl.BlockSpec((None, halo, width),
                         lambda i, j: (i, jnp.maximum(j * nh - 1, 0), col_blk)),
            pl.BlockSpec((None, halo, width),
                         lambda i, j: (i, jnp.minimum((j + 1) * nh, n_halo_blocks - 1), col_blk)),
        ]

    def const(shape):
        nd = len(shape)
        return pl.BlockSpec(shape, lambda i, j: (0,) * nd)

    in_specs = (seg_specs(0, dr) + seg_specs(1, dr) + seg_specs(2, dr) + seg_specs(lora_blk, LORA_PAD)
                + [const((1, dr)), const((1, dr)), const((1, dr)), const((1, LORA_PAD)),
                   const((2, dr)), const((2, dr)), const((2, LORA_PAD, dr)), const((2, LORA_PAD, dr)),
                   const((1, dr)), const((1, dr)), const((1, dr)), const((GROUP_W, GROUP_W))])
    tok = pl.BlockSpec((None, tt, dr), lambda i, j: (i, j, 0))
    tok2 = pl.BlockSpec((2, None, tt, dr), lambda i, j: (0, i, j, 0))
    sd = jax.ShapeDtypeStruct((b, t, dr), F32)
    sd2 = jax.ShapeDtypeStruct((2, b, t, dr), F32)
    args = [proj] * 12 + [lw["mu_r"], lw["mu_k"], lw["mu_v"], lw["mu_l"], lw["w0"], lw["a0"],
                          lw["w2w"], lw["w2a"], lw["k_k"], lw["k_a"], lw["r_k"], lw["bd"]]
    return pl.pallas_call(
        functools.partial(_prep_kernel, tt=tt, dr=dr, d_shift=d_shift, grid_mode=grid_mode, halo=halo),
        grid=(b, t // tt),
        in_specs=in_specs,
        out_specs=[tok,
                   pl.BlockSpec((None, tt // VX_STEPS, dr // 128, HEAD_DIM, 128),
                                lambda i, j: (i, j, 0, 0, 0)),
                   tok, tok, tok2, tok2, tok2],
        out_shape=[sd, jax.ShapeDtypeStruct((b, t // VX_STEPS, dr // 128, HEAD_DIM, 128), F32),
                   sd, sd, sd2, sd2, sd2],
        scratch_shapes=[pltpu.VMEM((tt + 2 * halo, dr), F32)] * 3
                       + [pltpu.VMEM((tt + 2 * halo, LORA_PAD), F32)],
        compiler_params=_cparams(("parallel", "parallel")),
        name="prep",
    )(*args)


def _scan_kernel(kap_ref, r_ref, vx_ref, w_ref, kt_ref, be_ref, s0_ref, bd_ref, e_ref,
                 o_ref, sf_ref, st, lhs, qbuf, oacc, *, ng, nb):
    d = pl.program_id(1)
    c = pl.program_id(2)

    @pl.when(c == 0)
    def _():
        st[...] = s0_ref[...]
        qbuf[...] = jnp.zeros_like(qbuf)

    oacc[...] = jnp.zeros_like(oacc)
    bd = bd_ref[...]
    tiles =[(s, g, (s * ng + g) * HEAD_DIM, slice(g * GROUP_W, (g + 1) * GROUP_W))
             for s in range(nb) for g in range(ng)]

    def column(j):
        return j + d * (OUT_BLK - 1 - 2 * j)

    def reduce_outputs(ecol):
        oc = jnp.dot(qbuf[...], e_ref[ecol], preferred_element_type=F32)
        for s, g, r0, _ in tiles:
            oacc[s, g] += oc[r0:r0 + HEAD_DIM]

    chunk = c + d * (pl.num_programs(2) - 1 - 2 * c)
    half = (chunk % (VX_STEPS // OUT_BLK)) * OUT_BLK
    lane = lax.broadcasted_iota(jnp.int32, (HEAD_DIM, 128), 1)

    def row(ref, s, t0, off, sl):
        return ref[s, pl.ds(t0, 8), sl][off:off + 1, :]

    t_first = column(0)
    for s, g, r0, sl in tiles:
        lhs[r0:r0 + HEAD_DIM, :] = st[s, g].astype(BF16) * kap_ref[s, pl.ds(t_first, 1), sl].astype(BF16)

    def eight_steps(blk, backward):
        n_tiles = OUT_BLK // 8
        t0 = pl.multiple_of(((n_tiles - 1 - blk) if backward else blk) * 8, 8)
        nxt = jnp.minimum(blk + 1, n_tiles - 1)
        t0_next = pl.multiple_of(((n_tiles - 1 - nxt) if backward else nxt) * 8, 8)
        for j in range(8):
            off = 7 - j if backward else j
            i = blk * 8 + j
            t = t0 + off
            if j < 7:
                tn0, off_next = t0, (off - 1 if backward else off + 1)
            else:
                tn0, off_next = t0_next, (7 if backward else 0)
            idx = jnp.where(lane < HEAD_DIM, half + t, VX_STEPS + half + t)
            res = jnp.dot(lhs[...], bd, preferred_element_type=F32)
            prev_col = (OUT_BLK - i) if backward else (i - 1)
            reduce_outputs(jnp.where(i == 0, OUT_BLK, prev_col))
            for s, g, r0, sl in tiles:
                s_new = (st[s, g] * row(w_ref, s, t0, off, sl)
                         - res[r0:r0 + HEAD_DIM] * row(be_ref, s, t0, off, sl)
                         + jnp.concatenate(
                             [jnp.take_along_axis(vx_ref[s, 2 * g + q], idx, axis=1) for q in range(2)],
                             axis=1) * row(kt_ref, s, t0, off, sl))
                st[s, g] = s_new
                sb = s_new.astype(BF16)
                qbuf[r0:r0 + HEAD_DIM, :] = sb * row(r_ref, s, t0, off, sl).astype(BF16)
                lhs[r0:r0 + HEAD_DIM, :] = sb * row(kap_ref, s, tn0, off_next, sl).astype(BF16)

    def block(blk, carry):
        lax.cond(d == 0, lambda: eight_steps(blk, False), lambda: eight_steps(blk, True))
        return carry

    lax.fori_loop(0, OUT_BLK // 8, block, 0)
    reduce_outputs(column(OUT_BLK - 1))
    for s, g, _, _ in tiles:
        ot = oacc[s, g].T
        for h in range(GROUP_HEADS):
            o_ref[s, g * GROUP_HEADS + h] = ot[h * OUT_BLK:(h + 1) * OUT_BLK, :]

    @pl.when(c == pl.num_programs(2) - 1)
    def _():
        sf_ref[...] = st[...]


def _scan(kap, r, vx, w, kt, be, s0, bd, e_all):
    b, t, dr = kap.shape
    ng = dr // GROUP_W
    h = dr // HEAD_DIM
    tc = OUT_BLK
    assert t % tc == 0
    nt = t // tc
    nb = _pick_tile(b, 1, SCAN_SEQS)
    rows = nb * ng * HEAD_DIM

    def cidx(dd, cc):
        return cc + dd * (nt - 1 - 2 * cc)

    tok = pl.BlockSpec((nb, tc, dr), lambda i, dd, cc: (i, cidx(dd, cc), 0))
    tok2 = pl.BlockSpec((None, nb, tc, dr), lambda i, dd, cc: (dd, i, cidx(dd, cc), 0))
    state = pl.BlockSpec((nb, None, ng, HEAD_DIM, GROUP_W), lambda i, dd, cc: (i, dd, 0, 0, 0))
    vxs = pl.BlockSpec((nb, None, 2 * ng, HEAD_DIM, 128),
                       lambda i, dd, cc: (i, cidx(dd, cc) // (VX_STEPS // OUT_BLK), 0, 0, 0))
    return pl.pallas_call(
        functools.partial(_scan_kernel, ng=ng, nb=nb),
        grid=(b // nb, 2, nt),
        in_specs=[tok, tok, vxs, tok2, tok2, tok2, state,
                  pl.BlockSpec((GROUP_W, GROUP_W), lambda i, dd, cc: (0, 0)),
                  pl.BlockSpec((OUT_BLK + 1, GROUP_W, 128), lambda i, dd, cc: (0, 0, 0))],
        out_specs=[
            pl.BlockSpec((None, nb, h, tc, HEAD_DIM), lambda i, dd, cc: (dd, i, 0, cidx(dd, cc), 0)),
            state,
        ],
        out_shape=[jax.ShapeDtypeStruct((2, b, h, t, HEAD_DIM), F32),
                   jax.ShapeDtypeStruct((b, 2, ng, HEAD_DIM, GROUP_W), F32)],
        scratch_shapes=[pltpu.VMEM((nb, ng, HEAD_DIM, GROUP_W), F32),
                        pltpu.VMEM((rows, GROUP_W), BF16),
                        pltpu.VMEM((rows, GROUP_W), BF16),
                        pltpu.VMEM((nb, ng, HEAD_DIM, 128), F32)],
        compiler_params=_cparams(("parallel", "parallel", "arbitrary")),
        name="rwkv7_scan",
    )(kap, r, vx, w, kt, be, s0, bd, e_all)


def _post_kernel(o_ref, bon_ref, g_ref, p_c, p_p, p_n, q_c, q_p, q_n, gc_ref,
                 lng_ref, lnb_ref, cw_ref, cb_ref, clg_ref, clb_ref,
                 x_o, zbuf, cbuf, *, tt, dr, kc, halo):
    i = pl.program_id(1)
    first = i == 0
    last = i == pl.num_programs(1) - 1
    pad = kc // 2

    def normed(h):
        o = o_ref[0, h] + o_ref[1, h]
        mu = jnp.mean(o, axis=-1, keepdims=True)
        var = jnp.mean(jnp.square(o - mu), axis=-1, keepdims=True)
        return (o - mu) * lax.rsqrt(var + GN_EPS)

    for hp in range(dr // 128):
        sl = slice(hp * 128, (hp + 1) * 128)
        on = jnp.concatenate([normed(2 * hp), normed(2 * hp + 1)], axis=-1)
        xr = (on * lng_ref[:, sl] + lnb_ref[:, sl] + bon_ref[:, sl]) * _silu(g_ref[:, sl])
        x_o[:, sl] = xr.astype(x_o.dtype)

    def glu(p, q):
        return p * jax.nn.sigmoid(q)

    zbuf[0:halo, :] = jnp.where(first, 0.0, glu(p_p[...], q_p[...]))
    zbuf[halo:halo + tt, :] = glu(p_c[...], q_c[...])
    zbuf[halo + tt:halo + tt + halo, :] = jnp.where(last, 0.0, glu(p_n[...], q_n[...]))

    for cg in range(dr // 128):
        sl = slice(cg * 128, (cg + 1) * 128)
        acc = jnp.zeros((tt, 128), F32)
        for j in range(kc):
            start = halo - pad + j
            acc = acc + zbuf[start:start + tt, sl] * cw_ref[j:j + 1, sl]
        cbuf[:, sl] = acc + cb_ref[:, sl]

    z = cbuf[...]
    zm = jnp.mean(z, axis=-1, keepdims=True)
    zv = jnp.mean(jnp.square(z - zm), axis=-1, keepdims=True)
    zn = (z - zm) * lax.rsqrt(zv + LN_EPS) * clg_ref[...] + clb_ref[...]
    x_o[:, dr:] = (_silu(zn) * _silu(gc_ref[...])).astype(x_o.dtype)


def _post(o, bonus, proj, lw):
    _, b, h, t, _ = o.shape
    dr = lw["dr"]
    kc = lw["conv_w"].shape[0]
    halo = 16
    assert kc // 2 <= halo
    tt = min(128, t)
    nh = tt // halo
    n_halo_blocks = t // halo

    def seg(col_blk):
        return pl.BlockSpec((None, tt, dr), lambda i, j: (i, j, col_blk))

    def seg_halo(col_blk):
        return [
            seg(col_blk),
            pl.BlockSpec((None, halo, dr), lambda i, j: (i, jnp.maximum(j * nh - 1, 0), col_blk)),
            pl.BlockSpec((None, halo, dr),
                         lambda i, j: (i, jnp.minimum((j + 1) * nh, n_halo_blocks - 1), col_blk)),
        ]

    def const(shape):
        nd = len(shape)
        return pl.BlockSpec(shape, lambda i, j: (0,) * nd)

    in_specs = ([pl.BlockSpec((2, None, h, tt, HEAD_DIM), lambda i, j: (0, i, 0, j, 0)),
                 pl.BlockSpec((None, tt, dr), lambda i, j: (i, j, 0)),
                 seg(3)] + seg_halo(4) + seg_halo(5) + [seg(6)]
                + [const((1, dr)), const((1, dr)), const((kc, dr)), const((1, dr)),
                   const((1, dr)), const((1, dr))])
    args = [o, bonus] + [proj] * 8 + [lw["lnx_g"], lw["lnx_b"], lw["conv_w"], lw["conv_b"],
                                      lw["cln_g"], lw["cln_b"]]
    return pl.pallas_call(
        functools.partial(_post_kernel, tt=tt, dr=dr, kc=kc, halo=halo),
        grid=(b, t // tt),
        in_specs=in_specs,
        out_specs=pl.BlockSpec((None, tt, 2 * dr), lambda i, j: (i, j, 0)),
        out_shape=jax.ShapeDtypeStruct((b, t, 2 * dr), BF16),
        scratch_shapes=[pltpu.VMEM((tt + 2 * halo, dr), F32), pltpu.VMEM((tt, dr), F32)],
        compiler_params=_cparams(("parallel", "parallel")),
        name="post",
    )(*args)


def _block_diag_ones():
    idx = np.arange(GROUP_W) // HEAD_DIM
    return jnp.asarray((idx[:, None] == idx[None, :]).astype(np.float32), dtype=BF16)


def _column_selectors():
    e = np.zeros((OUT_BLK + 1, GROUP_W, 128), np.float32)
    l = np.arange(GROUP_W)
    for j in range(OUT_BLK):
        e[j, l, (l // HEAD_DIM) * OUT_BLK + j] = 1.0
    return jnp.asarray(e, dtype=BF16)


def _layer_weights(l, w_in, mu_shift, decay_w0, decay_w2, iclr_a0, iclr_a2, k_k, k_a, r_k,
                   lnx_g, lnx_b, conv_w, conv_b, cln_g, cln_b, w_out, bd):
    d = w_in.shape[1]
    dr = k_k.shape[1]
    lw_ = decay_w2.shape[2]
    la_ = iclr_a2.shape[2]
    assert lw_ + la_ <= LORA_PAD and conv_w.shape[2] == dr and d == 2 * dr
    d_shift = 3 * dr + lw_ + la_
    wl = w_in[l]
    lora_cols = jnp.pad(wl[:, 3 * dr:d_shift], ((0, 0), (0, LORA_PAD - lw_ - la_)))
    w_perm = jnp.concatenate([wl[:, :3 * dr], wl[:, d_shift:], lora_cols], axis=1).astype(BF16)
    mu = mu_shift[l]
    w2w = jnp.zeros((2, LORA_PAD, dr), F32).at[:, :lw_].set(decay_w2[l]).astype(BF16)
    w2a = jnp.zeros((2, LORA_PAD, dr), F32).at[:, lw_:lw_ + la_].set(iclr_a2[l]).astype(BF16)
    row = lambda x: x.reshape(1, -1)
    return dict(
        dr=dr, d_shift=d_shift, w_in=w_perm, w_out=w_out[l].astype(BF16),
        mu_r=row(mu[:dr]), mu_k=row(mu[dr:2 * dr]), mu_v=row(mu[2 * dr:3 * dr]),
        mu_l=row(jnp.pad(mu[3 * dr:], (0, LORA_PAD - lw_ - la_))),
        w0=decay_w0[l], a0=iclr_a0[l], w2w=w2w, w2a=w2a,
        k_k=row(k_k[l]), k_a=row(k_a[l]), r_k=row(r_k[l]),
        lnx_g=row(lnx_g[l]), lnx_b=row(lnx_b[l]), conv_w=conv_w[l], conv_b=row(conv_b[l]),
        cln_g=row(cln_g[l]), cln_b=row(cln_b[l]), bd=bd)


def _to_tiles(s):
    b, two, h, n, _ = s.shape
    s = s.reshape(b, two, h // GROUP_HEADS, GROUP_HEADS, n, n)
    return jnp.transpose(s, (0, 1, 2, 4, 3, 5)).reshape(b, two, h // GROUP_HEADS, n, GROUP_W)


def _from_tiles(s):
    b, two, ng, n, _ = s.shape
    s = s.reshape(b, two, ng, n, GROUP_HEADS, n)
    return jnp.transpose(s, (0, 1, 2, 4, 3, 5)).reshape(b, two, ng * GROUP_HEADS, n, n)


def _mixer_layer(x, mod, s0_tiles, lw, norm_g, grid_mode, per_batch_row, e_all):
    b, t, d = x.shape
    h = _norm_mod(x, norm_g, mod, per_batch_row)
    proj = _in_proj(h.reshape(b * t, d), lw["w_in"]).reshape(b, t, -1)
    r, vx, kap, bonus, w, kt, be = _prep(proj, lw, grid_mode)
    o, s_fin = _scan(kap, r, vx, w, kt, be, s0_tiles, lw["bd"], e_all)
    xcat = _post(o, bonus, proj, lw)
    x_new = _out_proj(xcat.reshape(b * t, d), lw["w_out"], x.reshape(b * t, d), mod[:, 2 * d:],
                      t if per_batch_row else None).reshape(b, t, d)
    return x_new, s_fin


def kernel(x_prompt, x_sample, state_rwkv, c, c_ctx, norm_g, ada_w, ada_b, w_in, mu_shift,
           decay_w0, decay_w2, iclr_a0, iclr_a2, k_k, k_a, r_k, lnx_g, lnx_b, conv_w, conv_b,
           cln_g, cln_b, w_out, final_g):
    depth = w_in.shape[0]
    d = x_prompt.shape[-1]
    nb_ctx = x_prompt.shape[0]
    nb_lat = x_sample.shape[0]
    assert nb_lat + 1 <= 8
    dr = k_k.shape[1]
    hh = dr // HEAD_DIM

    cond8 = jnp.zeros((8, d), F32).at[0].set(c_ctx).at[1:1 + nb_lat].set(c)
    mod = _modulation(cond8, ada_w, ada_b)

    bd = _block_diag_ones()
    e_all = _column_selectors()
    zeros_state = jnp.zeros((nb_ctx, 2, hh // GROUP_HEADS, HEAD_DIM, GROUP_W), F32)
    r_k_flat = r_k.reshape(depth, dr)

    x_ctx, x_lat = x_prompt, x_sample
    new_states = []
    for l in range(depth):
        lw = _layer_weights(l, w_in, mu_shift, decay_w0, decay_w2, iclr_a0, iclr_a2, k_k, k_a,
                            r_k_flat, lnx_g, lnx_b, conv_w, conv_b, cln_g, cln_b, w_out, bd)
        x_ctx, s_fin = _mixer_layer(x_ctx, mod[l], zeros_state, lw, norm_g[l], False, False, e_all)
        new_states.append(_from_tiles(s_fin))
        x_lat, _ = _mixer_layer(x_lat, mod[l], _to_tiles(state_rwkv[:, l]), lw, norm_g[l],
                                True, True, e_all)
    y_prompt = _final_norm(x_ctx, final_g)
    y_sample = _final_norm(x_lat, final_g)
    return (y_prompt, y_sample, jnp.stack(new_states, axis=1))
```

```python
import functools

import numpy as np
import jax
import jax.numpy as jnp
from jax import lax
from jax.experimental import pallas as pl
from jax.experimental.pallas import tpu as pltpu

F32 = jnp.float32
BF16 = jnp.bfloat16

HEAD_DIM = 64
GRID_W = 64
LORA_PAD = 256
GROUP_HEADS = 4
GROUP_W = GROUP_HEADS * HEAD_DIM
OUT_BLK = 32
SCAN_SEQS = 4
VX_STEPS = 64
NORM_EPS = 1e-6
GN_EPS = 64e-5
LN_EPS = 1e-5
EXP_M05 = float(np.exp(-0.5))
VMEM_LIMIT_BYTES = 56 * 1024 * 1024


def _cparams(sem):
    return pltpu.CompilerParams(dimension_semantics=sem, vmem_limit_bytes=VMEM_LIMIT_BYTES)


def _silu(x):
    return x * jax.nn.sigmoid(x)


def _bdot(a, b):
    return jnp.dot(a.astype(BF16), b.astype(BF16), preferred_element_type=F32)


def _mod_kernel(c_ref, w_ref, b_ref, o_ref):
    o_ref[...] = _bdot(_silu(c_ref[...]), w_ref[...]) + b_ref[...]


def _modulation(cond8, ada_w, ada_b):
    depth, d, n3 = ada_w.shape
    tn = min(512, n3)
    return pl.pallas_call(
        _mod_kernel,
        grid=(depth, n3 // tn),
        in_specs=[
            pl.BlockSpec((8, d), lambda l, j: (0, 0)),
            pl.BlockSpec((None, d, tn), lambda l, j: (l, 0, j)),
            pl.BlockSpec((None, 1, tn), lambda l, j: (l, 0, j)),
        ],
        out_specs=pl.BlockSpec((None, 8, tn), lambda l, j: (l, 0, j)),
        out_shape=jax.ShapeDtypeStruct((depth, 8, n3), F32),
        compiler_params=_cparams(("parallel", "parallel")),
        name="modulation",
    )(cond8, ada_w, ada_b.reshape(depth, 1, n3))


def _norm_mod_kernel(x_ref, g_ref, mod_ref, h_ref, *, d, per_batch_row):
    row = (1 + pl.program_id(0)) if per_batch_row else 0
    x = x_ref[...]
    xn = x * lax.rsqrt(jnp.mean(x * x, axis=-1, keepdims=True) + NORM_EPS) * g_ref[...]
    m = mod_ref[pl.ds(row, 1), :]
    h_ref[...] = (xn * (1.0 + m[:, d:2 * d]) + m[:, :d]).astype(h_ref.dtype)


def _norm_mod(x, g, mod, per_batch_row):
    b, t, d = x.shape
    tt = min(256, t)
    return pl.pallas_call(
        functools.partial(_norm_mod_kernel, d=d, per_batch_row=per_batch_row),
        grid=(b, t // tt),
        in_specs=[
            pl.BlockSpec((None, tt, d), lambda i, j: (i, j, 0)),
            pl.BlockSpec((1, d), lambda i, j: (0, 0)),
            pl.BlockSpec((8, 3 * d), lambda i, j: (0, 0)),
        ],
        out_specs=pl.BlockSpec((None, tt, d), lambda i, j: (i, j, 0)),
        out_shape=jax.ShapeDtypeStruct((b, t, d), BF16),
        compiler_params=_cparams(("parallel", "parallel")),
        name="norm_mod",
    )(x, g.reshape(1, d), mod)


def _final_norm_kernel(x_ref, g_ref, o_ref):
    x = x_ref[...]
    o_ref[...] = x * lax.rsqrt(jnp.mean(x * x, axis=-1, keepdims=True) + NORM_EPS) * g_ref[...]


def _final_norm(x, g):
    b, t, d = x.shape
    tt = min(256, t)
    return pl.pallas_call(
        _final_norm_kernel,
        grid=(b, t // tt),
        in_specs=[
            pl.BlockSpec((None, tt, d), lambda i, j: (i, j, 0)),
            pl.BlockSpec((1, d), lambda i, j: (0, 0)),
        ],
        out_specs=pl.BlockSpec((None, tt, d), lambda i, j: (i, j, 0)),
        out_shape=jax.ShapeDtypeStruct((b, t, d), F32),
        compiler_params=_cparams(("parallel", "parallel")),
        name="final_norm",
    )(x, g.reshape(1, d))


def _mm_kernel(a_ref, w_ref, o_ref):
    o_ref[...] = jnp.dot(a_ref[...], w_ref[...], preferred_element_type=F32)


def _pick_tile(n, unit, cap):
    best = unit
    k = 1
    while k * unit <= min(n, cap):
        if n % (k * unit) == 0:
            best = k * unit
        k += 1
    return best


def _in_proj(h, w):
    m, k = h.shape
    n = w.shape[1]
    tm = _pick_tile(m, 256, 1024)
    tn = _pick_tile(n, 256, 768)
    return pl.pallas_call(
        _mm_kernel,
        grid=(m // tm, n // tn),
        in_specs=[
            pl.BlockSpec((tm, k), lambda i, j: (i, 0)),
            pl.BlockSpec((k, tn), lambda i, j: (0, j)),
        ],
        out_specs=pl.BlockSpec((tm, tn), lambda i, j: (i, j)),
        out_shape=jax.ShapeDtypeStruct((m, n), F32),
        compiler_params=_cparams(("parallel", "arbitrary")),
        name="in_proj",
    )(h, w)


def _out_proj_kernel(a_ref, w_ref, x_ref, gt_ref, o_ref, *, rows_per_batch, tm):
    if rows_per_batch is None:
        row = 0
    else:
        row = 1 + (pl.program_id(0) * tm) // rows_per_batch
    y = jnp.dot(a_ref[...], w_ref[...], preferred_element_type=F32)
    o_ref[...] = x_ref[...] + gt_ref[pl.ds(row, 1), :] * y


def _out_proj(a, w, x, gt8, rows_per_batch):
    m, k = a.shape
    n = w.shape[1]
    tm = _pick_tile(m, 256, 512)
    if rows_per_batch is not None:
        assert rows_per_batch % tm == 0
    tn = _pick_tile(n, 256, 1024)
    return pl.pallas_call(
        functools.partial(_out_proj_kernel, rows_per_batch=rows_per_batch, tm=tm),
        grid=(m // tm, n // tn),
        in_specs=[
            pl.BlockSpec((tm, k), lambda i, j: (i, 0)),
            pl.BlockSpec((k, tn), lambda i, j: (0, j)),
            pl.BlockSpec((tm, tn), lambda i, j: (i, j)),
            pl.BlockSpec((8, tn), lambda i, j: (0, j)),
        ],
        out_specs=pl.BlockSpec((tm, tn), lambda i, j: (i, j)),
        out_shape=jax.ShapeDtypeStruct((m, n), F32),
        compiler_params=_cparams(("parallel", "arbitrary")),
        name="out_proj",
    )(a, w, x, gt8)


def _shift_sources(seg_start, seg_width, d_shift, grid_mode):
    if grid_mode:
        q = d_shift // 4
        bounds = [(-1, 0, q), (1, q, 2 * q), (-GRID_W, 2 * q, 3 * q), (GRID_W, 3 * q, d_shift)]
    else:
        half = d_shift // 2
        bounds = [(-1, 0, half), (1, half, d_shift)]
    out = []
    for off, lo, hi in bounds:
        if lo < seg_start + seg_width and hi > seg_start:
            out.append((off, lo, hi))
    return out


def _prep_kernel(r_c, r_p, r_n, k_c, k_p, k_n, v_c, v_p, v_n, l_c, l_p, l_n,
                 mu_r, mu_k, mu_v, mu_l, w0_ref, a0_ref, w2w_ref, w2a_ref,
                 kk_ref, ka_ref, rk_ref, bd_ref,
                 r_o, vx_o, kap_o, bon_o, w_o, kt_o, be_o,
                 rbuf, kbuf, vbuf, lbuf, *, tt, dr, d_shift, grid_mode, halo):
    i = pl.program_id(1)
    nt = pl.num_programs(1)
    first = i == 0
    last = i == nt - 1

    def fill(buf, cur, prv, nxt):
        buf[0:halo, :] = jnp.where(first, 0.0, prv[...])
        buf[halo:halo + tt, :] = cur[...]
        buf[halo + tt:halo + tt + halo, :] = jnp.where(last, 0.0, nxt[...])

    fill(rbuf, r_c, r_p, r_n)
    fill(kbuf, k_c, k_p, k_n)
    fill(vbuf, v_c, v_p, v_n)
    fill(lbuf, l_c, l_p, l_n)

    rowcol = lax.broadcasted_iota(jnp.int32, (tt, 1), 0) % GRID_W

    def shifted(buf, mu_ref, seg_start, c0, cw):
        u = buf[halo:halo + tt, c0:c0 + cw]
        ch = seg_start + c0 + lax.broadcasted_iota(jnp.int32, (1, cw), 1)
        s = jnp.zeros_like(u)
        for off, lo, hi in _shift_sources(seg_start + c0, cw, d_shift, grid_mode):
            nb = buf[halo + off:halo + off + tt, c0:c0 + cw]
            ok = (ch >= lo) & (ch < hi)
            if grid_mode and off == -1:
                ok = ok & (rowcol != 0)
            if grid_mode and off == 1:
                ok = ok & (rowcol != GRID_W - 1)
            s = jnp.where(ok, nb, s)
        return u + mu_ref[:, c0:c0 + cw] * (s - u)

    lo_ = shifted(lbuf, mu_l, 3 * dr, 0, LORA_PAD)
    tw = jnp.tanh(lo_).astype(BF16)
    lo_b = lo_.astype(BF16)
    bd = bd_ref[...].astype(F32)

    lane128 = lax.broadcasted_iota(jnp.int32, (HEAD_DIM, 128), 1)

    for g in range(dr // GROUP_W):
        c0 = g * GROUP_W
        sl = slice(c0, c0 + GROUP_W)
        r = shifted(rbuf, mu_r, 0, c0, GROUP_W)
        k = shifted(kbuf, mu_k, dr, c0, GROUP_W)
        v = shifted(vbuf, mu_v, 2 * dr, c0, GROUP_W)
        kk = k * kk_ref[:, sl]
        ss = jnp.dot(kk * kk, bd, preferred_element_type=F32, precision=lax.Precision.HIGHEST)
        kap = kk * lax.rsqrt(jnp.maximum(ss, 1e-24))
        r_o[:, sl] = r
        kap_o[:, sl] = kap
        for q in range(GROUP_W // 128):
            vt = v[:, q * 128:(q + 1) * 128].T
            first, second = vt[:HEAD_DIM], vt[HEAD_DIM:]
            pair = g * (GROUP_W // 128) + q
            vx_o[0, pair] = jnp.where(lane128 < VX_STEPS, first, pltpu.roll(second, VX_STEPS, axis=1))
            vx_o[1, pair] = jnp.where(lane128 < VX_STEPS, pltpu.roll(first, VX_STEPS, axis=1), second)
        ktsum = jnp.zeros_like(k)
        for d in range(2):
            wl = w0_ref[d:d + 1, sl] + jnp.dot(tw, w2w_ref[d, :, sl], preferred_element_type=F32)
            w_o[d, :, sl] = jnp.exp(-(EXP_M05 * jax.nn.sigmoid(wl)))
            a = jax.nn.sigmoid(a0_ref[d:d + 1, sl]
                               + jnp.dot(lo_b, w2a_ref[d, :, sl], preferred_element_type=F32))
            kt = k * (1.0 + (a - 1.0) * ka_ref[:, sl])
            kt_o[d, :, sl] = kt
            be_o[d, :, sl] = kap * a
            ktsum = ktsum + kt
        cs = jnp.dot(r * ktsum * rk_ref[:, sl], bd, preferred_element_type=F32,
                     precision=lax.Precision.HIGHEST)
        bon_o[:, sl] = cs * v


def _prep(proj, lw, grid_mode):
    b, t, dp = proj.shape
    dr = lw["dr"]
    d_shift = lw["d_shift"]
    halo = GRID_W if grid_mode else 8
    tt = 2 * VX_STEPS
    assert tt % halo == 0 and t % tt == 0
    nh = tt // halo
    n_halo_blocks = t // halo
    lora_blk = (7 * dr) // LORA_PAD

    def seg_specs(col_blk, width):
        return [
            pl.BlockSpec((None, tt, width), lambda i, j: (i, j, col_blk)),
            pl.BlockSpec((None, halo, width),
                         lambda i, j: (i, jnp.maximum(j * nh - 1, 0), col_blk)),
            pl.BlockSpec((None, halo, width),
                         lambda i, j: (i, jnp.minimum((j + 1) * nh, n_halo_blocks - 1), col_blk)),
        ]

    def const(shape):
        nd = len(shape)
        return pl.BlockSpec(shape, lambda i, j: (0,) * nd)

    in_specs = (seg_specs(0, dr) + seg_specs(1, dr) + seg_specs(2, dr) + seg_specs(lora_blk, LORA_PAD)
                + [const((1, dr)), const((1, dr)), const((1, dr)), const((1, LORA_PAD)),
                   const((2, dr)), const((2, dr)), const((2, LORA_PAD, dr)), const((2, LORA_PAD, dr)),
                   const((1, dr)), const((1, dr)), const((1, dr)), const((GROUP_W, GROUP_W))])
    tok = pl.BlockSpec((None, tt, dr), lambda i, j: (i, j, 0))
    tok2 = pl.BlockSpec((2, None, tt, dr), lambda i, j: (0, i, j, 0))
    sd = jax.ShapeDtypeStruct((b, t, dr), F32)
    sd2 = jax.ShapeDtypeStruct((2, b, t, dr), F32)
    args = [proj] * 12 + [lw["mu_r"], lw["mu_k"], lw["mu_v"], lw["mu_l"], lw["w0"], lw["a0"],
                          lw["w2w"], lw["w2a"], lw["k_k"], lw["k_a"], lw["r_k"], lw["bd"]]
    return pl.pallas_call(
        functools.partial(_prep_kernel, tt=tt, dr=dr, d_shift=d_shift, grid_mode=grid_mode, halo=halo),
        grid=(b, t // tt),
        in_specs=in_specs,
        out_specs=[tok,
                   pl.BlockSpec((None, tt // VX_STEPS, dr // 128, HEAD_DIM, 128),
                                lambda i, j: (i, j, 0, 0, 0)),
                   tok, tok, tok2, tok2, tok2],
        out_shape=[sd, jax.ShapeDtypeStruct((b, t // VX_STEPS, dr // 128, HEAD_DIM, 128), F32),
                   sd, sd, sd2, sd2, sd2],
        scratch_shapes=[pltpu.VMEM((tt + 2 * halo, dr), F32)] * 3
                       + [pltpu.VMEM((tt + 2 * halo, LORA_PAD), F32)],
        compiler_params=_cparams(("parallel", "parallel")),
        name="prep",
    )(*args)


def _scan_kernel(kap_ref, r_ref, vx_ref, w_ref, kt_ref, be_ref, s0_ref, bd_ref, e_ref,
                 o_ref, sf_ref, st, lhs, qbuf, oacc, *, ng, nb):
    d = pl.program_id(1)
    c = pl.program_id(2)

    @pl.when(c == 0)
    def _():
        st[...] = s0_ref[...]
        qbuf[...] = jnp.zeros_like(qbuf)

    oacc[...] = jnp.zeros_like(oacc)
    bd = bd_ref[...]
    tiles =[(s, g, (s * ng + g) * HEAD_DIM, slice(g * GROUP_W, (g + 1) * GROUP_W))
             for s in range(nb) for g in range(ng)]

    def column(j):
        return j + d * (OUT_BLK - 1 - 2 * j)

    def reduce_outputs(ecol):
        oc = jnp.dot(qbuf[...], e_ref[ecol], preferred_element_type=F32)
        for s, g, r0, _ in tiles:
            oacc[s, g] += oc[r0:r0 + HEAD_DIM]

    chunk = c + d * (pl.num_programs(2) - 1 - 2 * c)
    half = (chunk % (VX_STEPS // OUT_BLK)) * OUT_BLK
    lane = lax.broadcasted_iota(jnp.int32, (HEAD_DIM, 128), 1)

    def row(ref, s, t0, off, sl):
        return ref[s, pl.ds(t0, 8), sl][off:off + 1, :]

    t_first = column(0)
    for s, g, r0, sl in tiles:
        lhs[r0:r0 + HEAD_DIM, :] = st[s, g].astype(BF16) * kap_ref[s, pl.ds(t_first, 1), sl].astype(BF16)

    def eight_steps(blk, backward):
        n_tiles = OUT_BLK // 8
        t0 = pl.multiple_of(((n_tiles - 1 - blk) if backward else blk) * 8, 8)
        nxt = jnp.minimum(blk + 1, n_tiles - 1)
        t0_next = pl.multiple_of(((n_tiles - 1 - nxt) if backward else nxt) * 8, 8)
        for j in range(8):
            off = 7 - j if backward else j
            i = blk * 8 + j
            t = t0 + off
            if j < 7:
                tn0, off_next = t0, (off - 1 if backward else off + 1)
            else:
                tn0, off_next = t0_next, (7 if backward else 0)
            idx = jnp.where(lane < HEAD_DIM, half + t, VX_STEPS + half + t)
            res = jnp.dot(lhs[...], bd, preferred_element_type=F32)
            prev_col = (OUT_BLK - i) if backward else (i - 1)
            reduce_outputs(jnp.where(i == 0, OUT_BLK, prev_col))
            for s, g, r0, sl in tiles:
                s_new = (st[s, g] * row(w_ref, s, t0, off, sl)
                         - res[r0:r0 + HEAD_DIM] * row(be_ref, s, t0, off, sl)
                         + jnp.concatenate(
                             [jnp.take_along_axis(vx_ref[s, 2 * g + q], idx, axis=1) for q in range(2)],
                             axis=1) * row(kt_ref, s, t0, off, sl))
                st[s, g] = s_new
                sb = s_new.astype(BF16)
                qbuf[r0:r0 + HEAD_DIM, :] = sb * row(r_ref, s, t0, off, sl).astype(BF16)
                lhs[r0:r0 + HEAD_DIM, :] = sb * row(kap_ref, s, tn0, off_next, sl).astype(BF16)

    def block(blk, carry):
        lax.cond(d == 0, lambda: eight_steps(blk, False), lambda: eight_steps(blk, True))
        return carry

    lax.fori_loop(0, OUT_BLK // 8, block, 0)
    reduce_outputs(column(OUT_BLK - 1))
    for s, g, _, _ in tiles:
        ot = oacc[s, g].T
        for h in range(GROUP_HEADS):
            o_ref[s, g * GROUP_HEADS + h] = ot[h * OUT_BLK:(h + 1) * OUT_BLK, :]

    @pl.when(c == pl.num_programs(2) - 1)
    def _():
        sf_ref[...] = st[...]


def _scan(kap, r, vx, w, kt, be, s0, bd, e_all):
    b, t, dr = kap.shape
    ng = dr // GROUP_W
    h = dr // HEAD_DIM
    tc = OUT_BLK
    assert t % tc == 0
    nt = t // tc
    nb = _pick_tile(b, 1, SCAN_SEQS)
    rows = nb * ng * HEAD_DIM

    def cidx(dd, cc):
        return cc + dd * (nt - 1 - 2 * cc)

    tok = pl.BlockSpec((nb, tc, dr), lambda i, dd, cc: (i, cidx(dd, cc), 0))
    tok2 = pl.BlockSpec((None, nb, tc, dr), lambda i, dd, cc: (dd, i, cidx(dd, cc), 0))
    state = pl.BlockSpec((nb, None, ng, HEAD_DIM, GROUP_W), lambda i, dd, cc: (i, dd, 0, 0, 0))
    vxs = pl.BlockSpec((nb, None, 2 * ng, HEAD_DIM, 128),
                       lambda i, dd, cc: (i, cidx(dd, cc) // (VX_STEPS // OUT_BLK), 0, 0, 0))
    return pl.pallas_call(
        functools.partial(_scan_kernel, ng=ng, nb=nb),
        grid=(b // nb, 2, nt),
        in_specs=[tok, tok, vxs, tok2, tok2, tok2, state,
                  pl.BlockSpec((GROUP_W, GROUP_W), lambda i, dd, cc: (0, 0)),
                  pl.BlockSpec((OUT_BLK + 1, GROUP_W, 128), lambda i, dd, cc: (0, 0, 0))],
        out_specs=[
            pl.BlockSpec((None, nb, h, tc, HEAD_DIM), lambda i, dd, cc: (dd, i, 0, cidx(dd, cc), 0)),
            state,
        ],
        out_shape=[jax.ShapeDtypeStruct((2, b, h, t, HEAD_DIM), F32),
                   jax.ShapeDtypeStruct((b, 2, ng, HEAD_DIM, GROUP_W), F32)],
        scratch_shapes=[pltpu.VMEM((nb, ng, HEAD_DIM, GROUP_W), F32),
                        pltpu.VMEM((rows, GROUP_W), BF16),
                        pltpu.VMEM((rows, GROUP_W), BF16),
                        pltpu.VMEM((nb, ng, HEAD_DIM, 128), F32)],
        compiler_params=_cparams(("parallel", "parallel", "arbitrary")),
        name="rwkv7_scan",
    )(kap, r, vx, w, kt, be, s0, bd, e_all)


def _post_kernel(o_ref, bon_ref, g_ref, p_c, p_p, p_n, q_c, q_p, q_n, gc_ref,
                 lng_ref, lnb_ref, cw_ref, cb_ref, clg_ref, clb_ref,
                 x_o, zbuf, cbuf, zsh, gn, *, tt, dr, kc, halo):
    i = pl.program_id(1)
    first = i == 0
    last = i == pl.num_programs(1) - 1
    pad = kc // 2

    n_heads = dr // HEAD_DIM
    for h in range(n_heads):
        o = o_ref[0, h] + o_ref[1, h]
        gn[0, h] = o
        gn[1, h] = jnp.broadcast_to(jnp.mean(o, axis=-1, keepdims=True), o.shape)
    for h in range(n_heads):
        dev = gn[0, h] - gn[1, h]
        var = jnp.mean(jnp.square(dev), axis=-1, keepdims=True)
        gn[2, h] = jnp.broadcast_to(lax.rsqrt(var + GN_EPS), dev.shape)

    def normed(h):
        return (gn[0, h] - gn[1, h]) * gn[2, h]

    for hp in range(dr // 128):
        sl = slice(hp * 128, (hp + 1) * 128)
        on = jnp.concatenate([normed(2 * hp), normed(2 * hp + 1)], axis=-1)
        xr = (on * lng_ref[:, sl] + lnb_ref[:, sl] + bon_ref[:, sl]) * _silu(g_ref[:, sl])
        x_o[:, sl] = xr.astype(x_o.dtype)

    def glu(p, q):
        return p * jax.nn.sigmoid(q)

    zbuf[0:halo, :] = jnp.where(first, 0.0, glu(p_p[...], q_p[...]))
    zbuf[halo:halo + tt, :] = glu(p_c[...], q_c[...])
    zbuf[halo + tt:halo + tt + halo, :] = jnp.where(last, 0.0, glu(p_n[...], q_n[...]))

    first_start = halo - pad
    n_rows = zsh.shape[1]
    for sh in range(8):
        zsh[sh] = zbuf[sh:sh + n_rows, :]
    for cg in range(dr // 128):
        sl = slice(cg * 128, (cg + 1) * 128)
        accs = [jnp.zeros((tt, 128), F32), jnp.zeros((tt, 128), F32)]
        for j in range(kc):
            start = first_start + j
            base = (start // 8) * 8
            accs[j % 2] = accs[j % 2] + zsh[start % 8, base:base + tt, sl] * cw_ref[j:j + 1, sl]
        cbuf[:, sl] = accs[0] + accs[1] + cb_ref[:, sl]

    z = cbuf[...]
    zm = jnp.mean(z, axis=-1, keepdims=True)
    zv = jnp.mean(jnp.square(z - zm), axis=-1, keepdims=True)
    zn = (z - zm) * lax.rsqrt(zv + LN_EPS) * clg_ref[...] + clb_ref[...]
    x_o[:, dr:] = (_silu(zn) * _silu(gc_ref[...])).astype(x_o.dtype)


def _post(o, bonus, proj, lw):
    _, b, h, t, _ = o.shape
    dr = lw["dr"]
    kc = lw["conv_w"].shape[0]
    halo = 16
    assert kc // 2 <= halo
    tt = min(128, t)
    nh = tt // halo
    n_halo_blocks = t // halo

    def seg(col_blk):
        return pl.BlockSpec((None, tt, dr), lambda i, j: (i, j, col_blk))

    def seg_halo(col_blk):
        return [
            seg(col_blk),
            pl.BlockSpec((None, halo, dr), lambda i, j: (i, jnp.maximum(j * nh - 1, 0), col_blk)),
            pl.BlockSpec((None, halo, dr),
                         lambda i, j: (i, jnp.minimum((j + 1) * nh, n_halo_blocks - 1), col_blk)),
        ]

    def const(shape):
        nd = len(shape)
        return pl.BlockSpec(shape, lambda i, j: (0,) * nd)

    in_specs = ([pl.BlockSpec((2, None, h, tt, HEAD_DIM), lambda i, j: (0, i, 0, j, 0)),
                 pl.BlockSpec((None, tt, dr), lambda i, j: (i, j, 0)),
                 seg(3)] + seg_halo(4) + seg_halo(5) + [seg(6)]
                + [const((1, dr)), const((1, dr)), const((kc, dr)), const((1, dr)),
                   const((1, dr)), const((1, dr))])
    args = [o, bonus] + [proj] * 8 + [lw["lnx_g"], lw["lnx_b"], lw["conv_w"], lw["conv_b"],
                                      lw["cln_g"], lw["cln_b"]]
    return pl.pallas_call(
        functools.partial(_post_kernel, tt=tt, dr=dr, kc=kc, halo=halo),
        grid=(b, t // tt),
        in_specs=in_specs,
        out_specs=pl.BlockSpec((None, tt, 2 * dr), lambda i, j: (i, j, 0)),
        out_shape=jax.ShapeDtypeStruct((b, t, 2 * dr), BF16),
        scratch_shapes=[pltpu.VMEM((tt + 2 * halo, dr), F32), pltpu.VMEM((tt, dr), F32),
                        pltpu.VMEM((8, tt + 2 * halo - 8, dr), F32),
                        pltpu.VMEM((3, h, tt, HEAD_DIM), F32)],
        compiler_params=_cparams(("parallel", "parallel")),
        name="post",
    )(*args)


def _block_diag_ones():
    idx = np.arange(GROUP_W) // HEAD_DIM
    return jnp.asarray((idx[:, None] == idx[None, :]).astype(np.float32), dtype=BF16)


def _column_selectors():
    e = np.zeros((OUT_BLK + 1, GROUP_W, 128), np.float32)
    l = np.arange(GROUP_W)
    for j in range(OUT_BLK):
        e[j, l, (l // HEAD_DIM) * OUT_BLK + j] = 1.0
    return jnp.asarray(e, dtype=BF16)


def _layer_weights(l, w_in, mu_shift, decay_w0, decay_w2, iclr_a0, iclr_a2, k_k, k_a, r_k,
                   lnx_g, lnx_b, conv_w, conv_b, cln_g, cln_b, w_out, bd):
    d = w_in.shape[1]
    dr = k_k.shape[1]
    lw_ = decay_w2.shape[2]
    la_ = iclr_a2.shape[2]
    assert lw_ + la_ <= LORA_PAD and conv_w.shape[2] == dr and d == 2 * dr
    d_shift = 3 * dr + lw_ + la_
    wl = w_in[l]
    lora_cols = jnp.pad(wl[:, 3 * dr:d_shift], ((0, 0), (0, LORA_PAD - lw_ - la_)))
    w_perm = jnp.concatenate([wl[:, :3 * dr], wl[:, d_shift:], lora_cols], axis=1).astype(BF16)
    mu = mu_shift[l]
    w2w = jnp.zeros((2, LORA_PAD, dr), F32).at[:, :lw_].set(decay_w2[l]).astype(BF16)
    w2a = jnp.zeros((2, LORA_PAD, dr), F32).at[:, lw_:lw_ + la_].set(iclr_a2[l]).astype(BF16)
    row = lambda x: x.reshape(1, -1)
    return dict(
        dr=dr, d_shift=d_shift, w_in=w_perm, w_out=w_out[l].astype(BF16),
        mu_r=row(mu[:dr]), mu_k=row(mu[dr:2 * dr]), mu_v=row(mu[2 * dr:3 * dr]),
        mu_l=row(jnp.pad(mu[3 * dr:], (0, LORA_PAD - lw_ - la_))),
        w0=decay_w0[l], a0=iclr_a0[l], w2w=w2w, w2a=w2a,
        k_k=row(k_k[l]), k_a=row(k_a[l]), r_k=row(r_k[l]),
        lnx_g=row(lnx_g[l]), lnx_b=row(lnx_b[l]), conv_w=conv_w[l], conv_b=row(conv_b[l]),
        cln_g=row(cln_g[l]), cln_b=row(cln_b[l]), bd=bd)


def _to_tiles(s):
    b, two, h, n, _ = s.shape
    s = s.reshape(b, two, h // GROUP_HEADS, GROUP_HEADS, n, n)
    return jnp.transpose(s, (0, 1, 2, 4, 3, 5)).reshape(b, two, h // GROUP_HEADS, n, GROUP_W)


def _from_tiles(s):
    b, two, ng, n, _ = s.shape
    s = s.reshape(b, two, ng, n, GROUP_HEADS, n)
    return jnp.transpose(s, (0, 1, 2, 4, 3, 5)).reshape(b, two, ng * GROUP_HEADS, n, n)


def _mixer_layer(x, mod, s0_tiles, lw, norm_g, grid_mode, per_batch_row, e_all):
    b, t, d = x.shape
    h = _norm_mod(x, norm_g, mod, per_batch_row)
    proj = _in_proj(h.reshape(b * t, d), lw["w_in"]).reshape(b, t, -1)
    r, vx, kap, bonus, w, kt, be = _prep(proj, lw, grid_mode)
    o, s_fin = _scan(kap, r, vx, w, kt, be, s0_tiles, lw["bd"], e_all)
    xcat = _post(o, bonus, proj, lw)
    x_new = _out_proj(xcat.reshape(b * t, d), lw["w_out"], x.reshape(b * t, d), mod[:, 2 * d:],
                      t if per_batch_row else None).reshape(b, t, d)
    return x_new, s_fin


def kernel(x_prompt, x_sample, state_rwkv, c, c_ctx, norm_g, ada_w, ada_b, w_in, mu_shift,
           decay_w0, decay_w2, iclr_a0, iclr_a2, k_k, k_a, r_k, lnx_g, lnx_b, conv_w, conv_b,
           cln_g, cln_b, w_out, final_g):
    depth = w_in.shape[0]
    d = x_prompt.shape[-1]
    nb_ctx = x_prompt.shape[0]
    nb_lat = x_sample.shape[0]
    assert nb_lat + 1 <= 8
    dr = k_k.shape[1]
    hh = dr // HEAD_DIM

    cond8 = jnp.zeros((8, d), F32).at[0].set(c_ctx).at[1:1 + nb_lat].set(c)
    mod = _modulation(cond8, ada_w, ada_b)

    bd = _block_diag_ones()
    e_all = _column_selectors()
    zeros_state = jnp.zeros((nb_ctx, 2, hh // GROUP_HEADS, HEAD_DIM, GROUP_W), F32)
    r_k_flat = r_k.reshape(depth, dr)

    x_ctx, x_lat = x_prompt, x_sample
    new_states = []
    for l in range(depth):
        lw = _layer_weights(l, w_in, mu_shift, decay_w0, decay_w2, iclr_a0, iclr_a2, k_k, k_a,
                            r_k_flat, lnx_g, lnx_b, conv_w, conv_b, cln_g, cln_b, w_out, bd)
        x_ctx, s_fin = _mixer_layer(x_ctx, mod[l], zeros_state, lw, norm_g[l], False, False, e_all)
        new_states.append(_from_tiles(s_fin))
        x_lat, _ = _mixer_layer(x_lat, mod[l], _to_tiles(state_rwkv[:, l]), lw, norm_g[l],
                                True, True, e_all)
    y_prompt = _final_norm(x_ctx, final_g)
    y_sample = _final_norm(x_lat, final_g)
    return (y_prompt, y_sample, jnp.stack(new_states, axis=1))
```

```python
import functools

import numpy as np
import jax
import jax.numpy as jnp
from jax import lax
from jax.experimental import pallas as pl
from jax.experimental.pallas import tpu as pltpu

F32 = jnp.float32
BF16 = jnp.bfloat16

HEAD_DIM = 64
GRID_W = 64
LORA_PAD = 256
GROUP_HEADS = 4
GROUP_W = GROUP_HEADS * HEAD_DIM
OUT_BLK = 32
SCAN_SEQS = 4
VX_STEPS = 64
NORM_EPS = 1e-6
GN_EPS = 64e-5
LN_EPS = 1e-5
EXP_M05 = float(np.exp(-0.5))
VMEM_LIMIT_BYTES = 56 * 1024 * 1024


def _cparams(sem):
    return pltpu.CompilerParams(dimension_semantics=sem, vmem_limit_bytes=VMEM_LIMIT_BYTES)


def _silu(x):
    return x * jax.nn.sigmoid(x)


def _bdot(a, b):
    return jnp.dot(a.astype(BF16), b.astype(BF16), preferred_element_type=F32)


def _mod_kernel(c_ref, w_ref, b_ref, o_ref):
    o_ref[...] = _bdot(_silu(c_ref[...]), w_ref[...]) + b_ref[...]


def _modulation(cond8, ada_w, ada_b):
    depth, d, n3 = ada_w.shape
    tn = min(512, n3)
    return pl.pallas_call(
        _mod_kernel,
        grid=(depth, n3 // tn),
        in_specs=[
            pl.BlockSpec((8, d), lambda l, j: (0, 0)),
            pl.BlockSpec((None, d, tn), lambda l, j: (l, 0, j)),
            pl.BlockSpec((None, 1, tn), lambda l, j: (l, 0, j)),
        ],
        out_specs=pl.BlockSpec((None, 8, tn), lambda l, j: (l, 0, j)),
        out_shape=jax.ShapeDtypeStruct((depth, 8, n3), F32),
        compiler_params=_cparams(("parallel", "parallel")),
        name="modulation",
    )(cond8, ada_w, ada_b.reshape(depth, 1, n3))


def _norm_mod_kernel(x_ref, g_ref, mod_ref, h_ref, *, d, per_batch_row):
    row = (1 + pl.program_id(0)) if per_batch_row else 0
    x = x_ref[...]
    xn = x * lax.rsqrt(jnp.mean(x * x, axis=-1, keepdims=True) + NORM_EPS) * g_ref[...]
    m = mod_ref[pl.ds(row, 1), :]
    h_ref[...] = (xn * (1.0 + m[:, d:2 * d]) + m[:, :d]).astype(h_ref.dtype)


def _norm_mod(x, g, mod, per_batch_row):
    b, t, d = x.shape
    tt = min(256, t)
    return pl.pallas_call(
        functools.partial(_norm_mod_kernel, d=d, per_batch_row=per_batch_row),
        grid=(b, t // tt),
        in_specs=[
            pl.BlockSpec((None, tt, d), lambda i, j: (i, j, 0)),
            pl.BlockSpec((1, d), lambda i, j: (0, 0)),
            pl.BlockSpec((8, 3 * d), lambda i, j: (0, 0)),
        ],
        out_specs=pl.BlockSpec((None, tt, d), lambda i, j: (i, j, 0)),
        out_shape=jax.ShapeDtypeStruct((b, t, d), BF16),
        compiler_params=_cparams(("parallel", "parallel")),
        name="norm_mod",
    )(x, g.reshape(1, d), mod)


def _final_norm_kernel(x_ref, g_ref, o_ref):
    x = x_ref[...]
    o_ref[...] = x * lax.rsqrt(jnp.mean(x * x, axis=-1, keepdims=True) + NORM_EPS) * g_ref[...]


def _final_norm(x, g):
    b, t, d = x.shape
    tt = min(256, t)
    return pl.pallas_call(
        _final_norm_kernel,
        grid=(b, t // tt),
        in_specs=[
            pl.BlockSpec((None, tt, d), lambda i, j: (i, j, 0)),
            pl.BlockSpec((1, d), lambda i, j: (0, 0)),
        ],
        out_specs=pl.BlockSpec((None, tt, d), lambda i, j: (i, j, 0)),
        out_shape=jax.ShapeDtypeStruct((b, t, d), F32),
        compiler_params=_cparams(("parallel", "parallel")),
        name="final_norm",
    )(x, g.reshape(1, d))


def _mm_kernel(a_ref, w_ref, o_ref):
    o_ref[...] = jnp.dot(a_ref[...], w_ref[...], preferred_element_type=F32)


def _pick_tile(n, unit, cap):
    best = unit
    k = 1
    while k * unit <= min(n, cap):
        if n % (k * unit) == 0:
            best = k * unit
        k += 1
    return best


def _in_proj(h, w_all, layer):
    m, k = h.shape
    n = w_all.shape[2]
    tm = _pick_tile(m, 256, 1024)
    tn = _pick_tile(n, 256, 768)
    return pl.pallas_call(
        _mm_kernel,
        grid=(m // tm, n // tn),
        in_specs=[
            pl.BlockSpec((tm, k), lambda i, j: (i, 0)),
            pl.BlockSpec((None, k, tn), lambda i, j: (layer, 0, j)),
        ],
        out_specs=pl.BlockSpec((tm, tn), lambda i, j: (i, j)),
        out_shape=jax.ShapeDtypeStruct((m, n), F32),
        compiler_params=_cparams(("parallel", "arbitrary")),
        name="in_proj",
    )(h, w_all)


def _out_proj_kernel(a_ref, w_ref, x_ref, gt_ref, o_ref, *, rows_per_batch, tm):
    if rows_per_batch is None:
        row = 0
    else:
        row = 1 + (pl.program_id(0) * tm) // rows_per_batch
    y = jnp.dot(a_ref[...], w_ref[...], preferred_element_type=F32)
    o_ref[...] = x_ref[...] + gt_ref[pl.ds(row, 1), :] * y


def _out_proj(a, w_all, layer, x, gt8, rows_per_batch):
    m, k = a.shape
    n = w_all.shape[2]
    tm = _pick_tile(m if rows_per_batch is None else rows_per_batch, 256, 1024)
    assert m % tm == 0
    tn = _pick_tile(n, 256, 1024)
    return pl.pallas_call(
        functools.partial(_out_proj_kernel, rows_per_batch=rows_per_batch, tm=tm),
        grid=(m // tm, n // tn),
        in_specs=[
            pl.BlockSpec((tm, k), lambda i, j: (i, 0)),
            pl.BlockSpec((None, k, tn), lambda i, j: (layer, 0, j)),
            pl.BlockSpec((tm, tn), lambda i, j: (i, j)),
            pl.BlockSpec((8, tn), lambda i, j: (0, j)),
        ],
        out_specs=pl.BlockSpec((tm, tn), lambda i, j: (i, j)),
        out_shape=jax.ShapeDtypeStruct((m, n), F32),
        compiler_params=_cparams(("parallel", "arbitrary")),
        name="out_proj",
    )(a, w_all, x, gt8)


def _shift_sources(seg_start, seg_width, d_shift, grid_mode):
    if grid_mode:
        q = d_shift // 4
        bounds = [(-1, 0, q), (1, q, 2 * q), (-GRID_W, 2 * q, 3 * q), (GRID_W, 3 * q, d_shift)]
    else:
        half = d_shift // 2
        bounds = [(-1, 0, half), (1, half, d_shift)]
    out = []
    for off, lo, hi in bounds:
        if lo < seg_start + seg_width and hi > seg_start:
            out.append((off, lo, hi))
    return out


def _prep_kernel(r_c, r_p, r_n, k_c, k_p, k_n, v_c, v_p, v_n, l_c, l_p, l_n,
                 mu_r, mu_k, mu_v, mu_l, w0_ref, a0_ref, w2w_ref, w2a_ref,
                 kk_ref, ka_ref, rk_ref, bd_ref,
                 r_o, vx_o, kap_o, bon_o, w_o, kt_o, be_o,
                 rbuf, kbuf, vbuf, lbuf, *, tt, dr, d_shift, grid_mode, halo):
    i = pl.program_id(1)
    nt = pl.num_programs(1)
    first = i == 0
    last = i == nt - 1

    def fill(buf, cur, prv, nxt):
        buf[0:halo, :] = jnp.where(first, 0.0, prv[...])
        buf[halo:halo + tt, :] = cur[...]
        buf[halo + tt:halo + tt + halo, :] = jnp.where(last, 0.0, nxt[...])

    fill(rbuf, r_c, r_p, r_n)
    fill(kbuf, k_c, k_p, k_n)
    fill(vbuf, v_c, v_p, v_n)
    fill(lbuf, l_c, l_p, l_n)

    rowcol = lax.broadcasted_iota(jnp.int32, (tt, 1), 0) % GRID_W

    def shifted(buf, mu_ref, seg_start, c0, cw):
        u = buf[halo:halo + tt, c0:c0 + cw]
        ch = seg_start + c0 + lax.broadcasted_iota(jnp.int32, (1, cw), 1)
        s = jnp.zeros_like(u)
        for off, lo, hi in _shift_sources(seg_start + c0, cw, d_shift, grid_mode):
            nb = buf[halo + off:halo + off + tt, c0:c0 + cw]
            ok = (ch >= lo) & (ch < hi)
            if grid_mode and off == -1:
                ok = ok & (rowcol != 0)
            if grid_mode and off == 1:
                ok = ok & (rowcol != GRID_W - 1)
            s = jnp.where(ok, nb, s)
        return u + mu_ref[:, c0:c0 + cw] * (s - u)

    lo_ = shifted(lbuf, mu_l, 3 * dr, 0, LORA_PAD)
    tw = jnp.tanh(lo_).astype(BF16)
    lo_b = lo_.astype(BF16)
    bd = bd_ref[...].astype(F32)

    lane128 = lax.broadcasted_iota(jnp.int32, (HEAD_DIM, 128), 1)

    for g in range(dr // GROUP_W):
        c0 = g * GROUP_W
        sl = slice(c0, c0 + GROUP_W)
        r = shifted(rbuf, mu_r, 0, c0, GROUP_W)
        k = shifted(kbuf, mu_k, dr, c0, GROUP_W)
        v = shifted(vbuf, mu_v, 2 * dr, c0, GROUP_W)
        kk = k * kk_ref[:, sl]
        ss = jnp.dot(kk * kk, bd, preferred_element_type=F32, precision=lax.Precision.HIGHEST)
        kap = kk * lax.rsqrt(jnp.maximum(ss, 1e-24))
        r_o[:, sl] = r
        kap_o[:, sl] = kap
        for q in range(GROUP_W // 128):
            vt = v[:, q * 128:(q + 1) * 128].T
            first, second = vt[:HEAD_DIM], vt[HEAD_DIM:]
            pair = g * (GROUP_W // 128) + q
            vx_o[0, pair] = jnp.where(lane128 < VX_STEPS, first, pltpu.roll(second, VX_STEPS, axis=1))
            vx_o[1, pair] = jnp.where(lane128 < VX_STEPS, pltpu.roll(first, VX_STEPS, axis=1), second)
        ktsum = jnp.zeros_like(k)
        for d in range(2):
            wl = w0_ref[d:d + 1, sl] + jnp.dot(tw, w2w_ref[d, :, sl], preferred_element_type=F32)
            w_o[d, :, sl] = jnp.exp(-(EXP_M05 * jax.nn.sigmoid(wl)))
            a = jax.nn.sigmoid(a0_ref[d:d + 1, sl]
                               + jnp.dot(lo_b, w2a_ref[d, :, sl], preferred_element_type=F32))
            kt = k * (1.0 + (a - 1.0) * ka_ref[:, sl])
            kt_o[d, :, sl] = kt
            be_o[d, :, sl] = kap * a
            ktsum = ktsum + kt
        cs = jnp.dot(r * ktsum * rk_ref[:, sl], bd, preferred_element_type=F32,
                     precision=lax.Precision.HIGHEST)
        bon_o[:, sl] = cs * v


def _prep(proj, lw, grid_mode):
    b, t, dp = proj.shape
    dr = lw["dr"]
    d_shift = lw["d_shift"]
    halo = GRID_W if grid_mode else 8
    tt = 2 * VX_STEPS
    assert tt % halo == 0 and t % tt == 0
    nh = tt // halo
    n_halo_blocks = t // halo
    lora_blk = (7 * dr) // LORA_PAD

    def seg_specs(col_blk, width):
        return [
            pl.BlockSpec((None, tt, width), lambda i, j: (i, j, col_blk)),
            pl.BlockSpec((None, halo, width),
                         lambda i, j: (i, jnp.maximum(j * nh - 1, 0), col_blk)),
            pl.BlockSpec((None, halo, width),
                         lambda i, j: (i, jnp.minimum((j + 1) * nh, n_halo_blocks - 1), col_blk)),
        ]

    def const(shape):
        nd = len(shape)
        return pl.BlockSpec(shape, lambda i, j: (0,) * nd)

    in_specs = (seg_specs(0, dr) + seg_specs(1, dr) + seg_specs(2, dr) + seg_specs(lora_blk, LORA_PAD)
                + [const((1, dr)), const((1, dr)), const((1, dr)), const((1, LORA_PAD)),
                   const((2, dr)), const((2, dr)), const((2, LORA_PAD, dr)), const((2, LORA_PAD, dr)),
                   const((1, dr)), const((1, dr)), const((1, dr)), const((GROUP_W, GROUP_W))])
    tok = pl.BlockSpec((None, tt, dr), lambda i, j: (i, j, 0))
    tok2 = pl.BlockSpec((2, None, tt, dr), lambda i, j: (0, i, j, 0))
    sd = jax.ShapeDtypeStruct((b, t, dr), F32)
    sd2 = jax.ShapeDtypeStruct((2, b, t, dr), F32)
    args = [proj] * 12 + [lw["mu_r"], lw["mu_k"], lw["mu_v"], lw["mu_l"], lw["w0"], lw["a0"],
                          lw["w2w"], lw["w2a"], lw["k_k"], lw["k_a"], lw["r_k"], lw["bd"]]
    return pl.pallas_call(
        functools.partial(_prep_kernel, tt=tt, dr=dr, d_shift=d_shift, grid_mode=grid_mode, halo=halo),
        grid=(b, t // tt),
        in_specs=in_specs,
        out_specs=[tok,
                   pl.BlockSpec((None, tt // VX_STEPS, dr // 128, HEAD_DIM, 128),
                                lambda i, j: (i, j, 0, 0, 0)),
                   tok, tok, tok2, tok2, tok2],
        out_shape=[sd, jax.ShapeDtypeStruct((b, t // VX_STEPS, dr // 128, HEAD_DIM, 128), F32),
                   sd, sd, sd2, sd2, sd2],
        scratch_shapes=[pltpu.VMEM((tt + 2 * halo, dr), F32)] * 3
                       + [pltpu.VMEM((tt + 2 * halo, LORA_PAD), F32)],
        compiler_params=_cparams(("parallel", "parallel")),
        name="prep",
    )(*args)


def _scan_kernel(kap_ref, r_ref, vx_ref, w_ref, kt_ref, be_ref, s0_ref, bd_ref, e_ref,
                 o_ref, sf_ref, st, lhs, qbuf, oacc, *, ng, nb):
    d = pl.program_id(1)
    c = pl.program_id(2)

    @pl.when(c == 0)
    def _():
        st[...] = s0_ref[...]
        qbuf[...] = jnp.zeros_like(qbuf)

    oacc[...] = jnp.zeros_like(oacc)
    bd = bd_ref[...]
    tiles = [(s, g, (s * ng + g) * HEAD_DIM, slice(g * GROUP_W, (g + 1) * GROUP_W))
             for s in range(nb) for g in range(ng)]

    def column(j):
        return j + d * (OUT_BLK - 1 - 2 * j)

    def reduce_outputs(ecol):
        oc = jnp.dot(qbuf[...], e_ref[ecol], preferred_element_type=F32)
        for s, g, r0, _ in tiles:
            oacc[s, g] += oc[r0:r0 + HEAD_DIM]

    chunk = c + d * (pl.num_programs(2) - 1 - 2 * c)
    half = (chunk % (VX_STEPS // OUT_BLK)) * OUT_BLK
    lane = lax.broadcasted_iota(jnp.int32, (HEAD_DIM, 128), 1)

    def row(ref, s, t0, off, sl):
        return ref[s, pl.ds(t0, 8), sl][off:off + 1, :]

    t_first = column(0)
    for s, g, r0, sl in tiles:
        lhs[r0:r0 + HEAD_DIM, :] = st[s, g].astype(BF16) * kap_ref[s, pl.ds(t_first, 1), sl].astype(BF16)

    def eight_steps(blk, backward):
        n_tiles = OUT_BLK // 8
        t0 = pl.multiple_of(((n_tiles - 1 - blk) if backward else blk) * 8, 8)
        nxt = jnp.minimum(blk + 1, n_tiles - 1)
        t0_next = pl.multiple_of(((n_tiles - 1 - nxt) if backward else nxt) * 8, 8)
        for j in range(8):
            off = 7 - j if backward else j
            i = blk * 8 + j
            t = t0 + off
            if j < 7:
                tn0, off_next = t0, (off - 1 if backward else off + 1)
            else:
                tn0, off_next = t0_next, (7 if backward else 0)
            idx = jnp.where(lane < HEAD_DIM, half + t, VX_STEPS + half + t)
            res = jnp.dot(lhs[...], bd, preferred_element_type=F32)
            prev_col = (OUT_BLK - i) if backward else (i - 1)
            reduce_outputs(jnp.where(i == 0, OUT_BLK, prev_col))
            for s, g, r0, sl in tiles:
                s_new = (st[s, g] * row(w_ref, s, t0, off, sl)
                         - res[r0:r0 + HEAD_DIM] * row(be_ref, s, t0, off, sl)
                         + jnp.concatenate(
                             [jnp.take_along_axis(vx_ref[s, 2 * g + q], idx, axis=1) for q in range(2)],
                             axis=1) * row(kt_ref, s, t0, off, sl))
                st[s, g] = s_new
                sb = s_new.astype(BF16)
                qbuf[r0:r0 + HEAD_DIM, :] = sb * row(r_ref, s, t0, off, sl).astype(BF16)
                lhs[r0:r0 + HEAD_DIM, :] = sb * row(kap_ref, s, tn0, off_next, sl).astype(BF16)

    def block(blk, carry):
        lax.cond(d == 0, lambda: eight_steps(blk, False), lambda: eight_steps(blk, True))
        return carry

    lax.fori_loop(0, OUT_BLK // 8, block, 0)
    reduce_outputs(column(OUT_BLK - 1))
    for s, g, _, _ in tiles:
        ot = oacc[s, g].T
        for h in range(GROUP_HEADS):
            o_ref[s, g * GROUP_HEADS + h] = ot[h * OUT_BLK:(h + 1) * OUT_BLK, :]

    @pl.when(c == pl.num_programs(2) - 1)
    def _():
        sf_ref[...] = st[...]


def _scan(kap, r, vx, w, kt, be, s0, bd, e_all):
    b, t, dr = kap.shape
    ng = dr // GROUP_W
    h = dr // HEAD_DIM
    tc = OUT_BLK
    assert t % tc == 0
    nt = t // tc
    nb = _pick_tile(b, 1, SCAN_SEQS)
    rows = nb * ng * HEAD_DIM

    def cidx(dd, cc):
        return cc + dd * (nt - 1 - 2 * cc)

    tok = pl.BlockSpec((nb, tc, dr), lambda i, dd, cc: (i, cidx(dd, cc), 0))
    tok2 = pl.BlockSpec((None, nb, tc, dr), lambda i, dd, cc: (dd, i, cidx(dd, cc), 0))
    state = pl.BlockSpec((nb, None, ng, HEAD_DIM, GROUP_W), lambda i, dd, cc: (i, dd, 0, 0, 0))
    vxs = pl.BlockSpec((nb, None, 2 * ng, HEAD_DIM, 128),
                       lambda i, dd, cc: (i, cidx(dd, cc) // (VX_STEPS // OUT_BLK), 0, 0, 0))
    return pl.pallas_call(
        functools.partial(_scan_kernel, ng=ng, nb=nb),
        grid=(b // nb, 2, nt),
        in_specs=[tok, tok, vxs, tok2, tok2, tok2, state,
                  pl.BlockSpec((GROUP_W, GROUP_W), lambda i, dd, cc: (0, 0)),
                  pl.BlockSpec((OUT_BLK + 1, GROUP_W, 128), lambda i, dd, cc: (0, 0, 0))],
        out_specs=[
            pl.BlockSpec((None, nb, h, tc, HEAD_DIM), lambda i, dd, cc: (dd, i, 0, cidx(dd, cc), 0)),
            state,
        ],
        out_shape=[jax.ShapeDtypeStruct((2, b, h, t, HEAD_DIM), F32),
                   jax.ShapeDtypeStruct((b, 2, ng, HEAD_DIM, GROUP_W), F32)],
        scratch_shapes=[pltpu.VMEM((nb, ng, HEAD_DIM, GROUP_W), F32),
                        pltpu.VMEM((rows, GROUP_W), BF16),
                        pltpu.VMEM((rows, GROUP_W), BF16),
                        pltpu.VMEM((nb, ng, HEAD_DIM, 128), F32)],
        compiler_params=_cparams(("parallel", "parallel", "arbitrary")),
        name="rwkv7_scan",
    )(kap, r, vx, w, kt, be, s0, bd, e_all)


def _post_kernel(o_ref, bon_ref, g_ref, p_c, p_p, p_n, q_c, q_p, q_n, gc_ref,
                 lng_ref, lnb_ref, cw_ref, cb_ref, clg_ref, clb_ref,
                 x_o, zbuf, cbuf, zsh, gn, *, tt, dr, kc, halo):
    i = pl.program_id(1)
    first = i == 0
    last = i == pl.num_programs(1) - 1
    pad = kc // 2

    n_heads = dr // HEAD_DIM
    for h in range(n_heads):
        o = o_ref[0, h] + o_ref[1, h]
        gn[0, h] = o
        gn[1, h] = jnp.broadcast_to(jnp.mean(o, axis=-1, keepdims=True), o.shape)
    for h in range(n_heads):
        dev = gn[0, h] - gn[1, h]
        var = jnp.mean(jnp.square(dev), axis=-1, keepdims=True)
        gn[2, h] = jnp.broadcast_to(lax.rsqrt(var + GN_EPS), dev.shape)

    def normed(h):
        return (gn[0, h] - gn[1, h]) * gn[2, h]

    for hp in range(dr // 128):
        sl = slice(hp * 128, (hp + 1) * 128)
        on = jnp.concatenate([normed(2 * hp), normed(2 * hp + 1)], axis=-1)
        xr = (on * lng_ref[:, sl] + lnb_ref[:, sl] + bon_ref[:, sl]) * _silu(g_ref[:, sl])
        x_o[:, sl] = xr.astype(x_o.dtype)

    def glu(p, q):
        return p * jax.nn.sigmoid(q)

    zbuf[0:halo, :] = jnp.where(first, 0.0, glu(p_p[...], q_p[...]))
    zbuf[halo:halo + tt, :] = glu(p_c[...], q_c[...])
    zbuf[halo + tt:halo + tt + halo, :] = jnp.where(last, 0.0, glu(p_n[...], q_n[...]))

    first_start = halo - pad
    n_rows = zsh.shape[1]
    for sh in range(8):
        zsh[sh] = zbuf[sh:sh + n_rows, :]
    for cg in range(dr // 128):
        sl = slice(cg * 128, (cg + 1) * 128)
        accs = [jnp.zeros((tt, 128), F32), jnp.zeros((tt, 128), F32)]
        for j in range(kc):
            start = first_start + j
            base = (start // 8) * 8
            accs[j % 2] = accs[j % 2] + zsh[start % 8, base:base + tt, sl] * cw_ref[j:j + 1, sl]
        cbuf[:, sl] = accs[0] + accs[1] + cb_ref[:, sl]

    z = cbuf[...]
    zm = jnp.mean(z, axis=-1, keepdims=True)
    zv = jnp.mean(jnp.square(z - zm), axis=-1, keepdims=True)
    zn = (z - zm) * lax.rsqrt(zv + LN_EPS) * clg_ref[...] + clb_ref[...]
    x_o[:, dr:] = (_silu(zn) * _silu(gc_ref[...])).astype(x_o.dtype)


def _post(o, bonus, proj, lw):
    _, b, h, t, _ = o.shape
    dr = lw["dr"]
    kc = lw["conv_w"].shape[0]
    halo = 16
    assert kc // 2 <= halo
    tt = min(128, t)
    nh = tt // halo
    n_halo_blocks = t // halo

    def seg(col_blk):
        return pl.BlockSpec((None, tt, dr), lambda i, j: (i, j, col_blk))

    def seg_halo(col_blk):
        return [
            seg(col_blk),
            pl.BlockSpec((None, halo, dr), lambda i, j: (i, jnp.maximum(j * nh - 1, 0), col_blk)),
            pl.BlockSpec((None, halo, dr),
                         lambda i, j: (i, jnp.minimum((j + 1) * nh, n_halo_blocks - 1), col_blk)),
        ]

    def const(shape):
        nd = len(shape)
        return pl.BlockSpec(shape, lambda i, j: (0,) * nd)

    in_specs = ([pl.BlockSpec((2, None, h, tt, HEAD_DIM), lambda i, j: (0, i, 0, j, 0)),
                 pl.BlockSpec((None, tt, dr), lambda i, j: (i, j, 0)),
                 seg(3)] + seg_halo(4) + seg_halo(5) + [seg(6)]
                + [const((1, dr)), const((1, dr)), const((kc, dr)), const((1, dr)),
                   const((1, dr)), const((1, dr))])
    args = [o, bonus] + [proj] * 8 + [lw["lnx_g"], lw["lnx_b"], lw["conv_w"], lw["conv_b"],
                                      lw["cln_g"], lw["cln_b"]]
    return pl.pallas_call(
        functools.partial(_post_kernel, tt=tt, dr=dr, kc=kc, halo=halo),
        grid=(b, t // tt),
        in_specs=in_specs,
        out_specs=pl.BlockSpec((None, tt, 2 * dr), lambda i, j: (i, j, 0)),
        out_shape=jax.ShapeDtypeStruct((b, t, 2 * dr), BF16),
        scratch_shapes=[pltpu.VMEM((tt + 2 * halo, dr), F32), pltpu.VMEM((tt, dr), F32),
                        pltpu.VMEM((8, tt + 2 * halo - 8, dr), F32),
                        pltpu.VMEM((3, h, tt, HEAD_DIM), F32)],
        compiler_params=_cparams(("parallel", "parallel")),
        name="post",
    )(*args)


def _block_diag_ones():
    idx = np.arange(GROUP_W) // HEAD_DIM
    return jnp.asarray((idx[:, None] == idx[None, :]).astype(np.float32), dtype=BF16)


def _column_selectors():
    e = np.zeros((OUT_BLK + 1, GROUP_W, 128), np.float32)
    l = np.arange(GROUP_W)
    for j in range(OUT_BLK):
        e[j, l, (l // HEAD_DIM) * OUT_BLK + j] = 1.0
    return jnp.asarray(e, dtype=BF16)


def _projection_weights(w_in, w_out, dr, d_shift):
    lora_cols = jnp.pad(w_in[:, :, 3 * dr:d_shift], ((0, 0), (0, 0), (0, LORA_PAD - (d_shift - 3 * dr))))
    w_in_all = jnp.concatenate([w_in[:, :, :3 * dr], w_in[:, :, d_shift:], lora_cols], axis=2)
    return w_in_all.astype(BF16), w_out.astype(BF16)


def _layer_weights(l, w_in, mu_shift, decay_w0, decay_w2, iclr_a0, iclr_a2, k_k, k_a, r_k,
                   lnx_g, lnx_b, conv_w, conv_b, cln_g, cln_b, bd):
    d = w_in.shape[1]
    dr = k_k.shape[1]
    lw_ = decay_w2.shape[2]
    la_ = iclr_a2.shape[2]
    assert lw_ + la_ <= LORA_PAD and conv_w.shape[2] == dr and d == 2 * dr
    d_shift = 3 * dr + lw_ + la_
    mu = mu_shift[l]
    w2w = jnp.zeros((2, LORA_PAD, dr), F32).at[:, :lw_].set(decay_w2[l]).astype(BF16)
    w2a = jnp.zeros((2, LORA_PAD, dr), F32).at[:, lw_:lw_ + la_].set(iclr_a2[l]).astype(BF16)
    row = lambda x: x.reshape(1, -1)
    return dict(
        layer=l, dr=dr, d_shift=d_shift,
        mu_r=row(mu[:dr]), mu_k=row(mu[dr:2 * dr]), mu_v=row(mu[2 * dr:3 * dr]),
        mu_l=row(jnp.pad(mu[3 * dr:], (0, LORA_PAD - lw_ - la_))),
        w0=decay_w0[l], a0=iclr_a0[l], w2w=w2w, w2a=w2a,
        k_k=row(k_k[l]), k_a=row(k_a[l]), r_k=row(r_k[l]),
        lnx_g=row(lnx_g[l]), lnx_b=row(lnx_b[l]), conv_w=conv_w[l], conv_b=row(conv_b[l]),
        cln_g=row(cln_g[l]), cln_b=row(cln_b[l]), bd=bd)


def _to_tiles(s):
    b, two, h, n, _ = s.shape
    s = s.reshape(b, two, h // GROUP_HEADS, GROUP_HEADS, n, n)
    return jnp.transpose(s, (0, 1, 2, 4, 3, 5)).reshape(b, two, h // GROUP_HEADS, n, GROUP_W)


def _from_tiles(s):
    b, two, ng, n, _ = s.shape
    s = s.reshape(b, two, ng, n, GROUP_HEADS, n)
    return jnp.transpose(s, (0, 1, 2, 4, 3, 5)).reshape(b, two, ng * GROUP_HEADS, n, n)


def _mixer_layer(x, mod, s0_tiles, lw, w_in_all, w_out_all, norm_g, grid_mode, per_batch_row, e_all):
    b, t, d = x.shape
    h = _norm_mod(x, norm_g, mod, per_batch_row)
    proj = _in_proj(h.reshape(b * t, d), w_in_all, lw["layer"]).reshape(b, t, -1)
    r, vx, kap, bonus, w, kt, be = _prep(proj, lw, grid_mode)
    o, s_fin = _scan(kap, r, vx, w, kt, be, s0_tiles, lw["bd"], e_all)
    xcat = _post(o, bonus, proj, lw)
    x_new = _out_proj(xcat.reshape(b * t, d), w_out_all, lw["layer"], x.reshape(b * t, d),
                      mod[:, 2 * d:], t if per_batch_row else None).reshape(b, t, d)
    return x_new, s_fin


def kernel(x_prompt, x_sample, state_rwkv, c, c_ctx, norm_g, ada_w, ada_b, w_in, mu_shift,
           decay_w0, decay_w2, iclr_a0, iclr_a2, k_k, k_a, r_k, lnx_g, lnx_b, conv_w, conv_b,
           cln_g, cln_b, w_out, final_g):
    depth = w_in.shape[0]
    d = x_prompt.shape[-1]
    nb_ctx = x_prompt.shape[0]
    nb_lat = x_sample.shape[0]
    assert nb_lat + 1 <= 8
    dr = k_k.shape[1]
    hh = dr // HEAD_DIM

    cond8 = jnp.zeros((8, d), F32).at[0].set(c_ctx).at[1:1 + nb_lat].set(c)
    mod = _modulation(cond8, ada_w, ada_b)

    bd = _block_diag_ones()
    e_all = _column_selectors()
    zeros_state = jnp.zeros((nb_ctx, 2, hh // GROUP_HEADS, HEAD_DIM, GROUP_W), F32)
    r_k_flat = r_k.reshape(depth, dr)
    d_shift = 3 * dr + decay_w2.shape[2] + iclr_a2.shape[2]
    w_in_all, w_out_all = _projection_weights(w_in, w_out, dr, d_shift)

    x_ctx, x_lat = x_prompt, x_sample
    new_states = []
    for l in range(depth):
        lw = _layer_weights(l, w_in, mu_shift, decay_w0, decay_w2, iclr_a0, iclr_a2, k_k, k_a,
                            r_k_flat, lnx_g, lnx_b, conv_w, conv_b, cln_g, cln_b, bd)
        x_ctx, s_fin = _mixer_layer(x_ctx, mod[l], zeros_state, lw, w_in_all, w_out_all, norm_g[l],
                                    False, False, e_all)
        new_states.append(_from_tiles(s_fin))
        x_lat, _ = _mixer_layer(x_lat, mod[l], _to_tiles(state_rwkv[:, l]), lw, w_in_all, w_out_all,
                                norm_g[l], True, True, e_all)
    y_prompt = _final_norm(x_ctx, final_g)
    y_sample = _final_norm(x_lat, final_g)
    return (y_prompt, y_sample, jnp.stack(new_states, axis=1))
```

```python
import functools

import numpy as np
import jax
import jax.numpy as jnp
from jax import lax
from jax.experimental import pallas as pl
from jax.experimental.pallas import tpu as pltpu

F32 = jnp.float32
BF16 = jnp.bfloat16

HEAD_DIM = 64
GRID_W = 64
LORA_PAD = 256
GROUP_HEADS = 4
GROUP_W = GROUP_HEADS * HEAD_DIM
OUT_BLK = 32
SCAN_SEQS = 2
VX_STEPS = 64
NORM_EPS = 1e-6
GN_EPS = 64e-5
LN_EPS = 1e-5
EXP_M05 = float(np.exp(-0.5))
VMEM_LIMIT_BYTES = 56 * 1024 * 1024


def _cparams(sem):
    return pltpu.CompilerParams(dimension_semantics=sem, vmem_limit_bytes=VMEM_LIMIT_BYTES)


def _silu(x):
    return x * jax.nn.sigmoid(x)


def _bdot(a, b):
    return jnp.dot(a.astype(BF16), b.astype(BF16), preferred_element_type=F32)


def _mod_kernel(c_ref, w_ref, b_ref, o_ref):
    o_ref[...] = _bdot(_silu(c_ref[...]), w_ref[...]) + b_ref[...]


def _modulation(cond8, ada_w, ada_b):
    depth, d, n3 = ada_w.shape
    tn = min(512, n3)
    return pl.pallas_call(
        _mod_kernel,
        grid=(depth, n3 // tn),
        in_specs=[
            pl.BlockSpec((8, d), lambda l, j: (0, 0)),
            pl.BlockSpec((None, d, tn), lambda l, j: (l, 0, j)),
            pl.BlockSpec((None, 1, tn), lambda l, j: (l, 0, j)),
        ],
        out_specs=pl.BlockSpec((None, 8, tn), lambda l, j: (l, 0, j)),
        out_shape=jax.ShapeDtypeStruct((depth, 8, n3), F32),
        compiler_params=_cparams(("parallel", "parallel")),
        name="modulation",
    )(cond8, ada_w, ada_b.reshape(depth, 1, n3))


def _norm_mod_kernel(x_ref, g_ref, mod_ref, h_ref, *, d, per_batch_row):
    row = (1 + pl.program_id(0)) if per_batch_row else 0
    x = x_ref[...]
    xn = x * lax.rsqrt(jnp.mean(x * x, axis=-1, keepdims=True) + NORM_EPS) * g_ref[...]
    m = mod_ref[pl.ds(row, 1), :]
    h_ref[...] = (xn * (1.0 + m[:, d:2 * d]) + m[:, :d]).astype(h_ref.dtype)


def _norm_mod(x, g, mod, per_batch_row):
    b, t, d = x.shape
    tt = min(256, t)
    return pl.pallas_call(
        functools.partial(_norm_mod_kernel, d=d, per_batch_row=per_batch_row),
        grid=(b, t // tt),
        in_specs=[
            pl.BlockSpec((None, tt, d), lambda i, j: (i, j, 0)),
            pl.BlockSpec((1, d), lambda i, j: (0, 0)),
            pl.BlockSpec((8, 3 * d), lambda i, j: (0, 0)),
        ],
        out_specs=pl.BlockSpec((None, tt, d), lambda i, j: (i, j, 0)),
        out_shape=jax.ShapeDtypeStruct((b, t, d), BF16),
        compiler_params=_cparams(("parallel", "parallel")),
        name="norm_mod",
    )(x, g.reshape(1, d), mod)


def _final_norm_kernel(x_ref, g_ref, o_ref):
    x = x_ref[...]
    o_ref[...] = x * lax.rsqrt(jnp.mean(x * x, axis=-1, keepdims=True) + NORM_EPS) * g_ref[...]


def _final_norm(x, g):
    b, t, d = x.shape
    tt = min(256, t)
    return pl.pallas_call(
        _final_norm_kernel,
        grid=(b, t // tt),
        in_specs=[
            pl.BlockSpec((None, tt, d), lambda i, j: (i, j, 0)),
            pl.BlockSpec((1, d), lambda i, j: (0, 0)),
        ],
        out_specs=pl.BlockSpec((None, tt, d), lambda i, j: (i, j, 0)),
        out_shape=jax.ShapeDtypeStruct((b, t, d), F32),
        compiler_params=_cparams(("parallel", "parallel")),
        name="final_norm",
    )(x, g.reshape(1, d))


def _mm_kernel(a_ref, w_ref, o_ref):
    o_ref[...] = jnp.dot(a_ref[...], w_ref[...], preferred_element_type=F32)


def _pick_tile(n, unit, cap):
    best = unit
    k = 1
    while k * unit <= min(n, cap):
        if n % (k * unit) == 0:
            best = k * unit
        k += 1
    return best


def _in_proj(h, w_all, layer):
    m, k = h.shape
    n = w_all.shape[2]
    tm = _pick_tile(m, 256, 1024)
    tn = _pick_tile(n, 256, 768)
    return pl.pallas_call(
        _mm_kernel,
        grid=(m // tm, n // tn),
        in_specs=[
            pl.BlockSpec((tm, k), lambda i, j: (i, 0)),
            pl.BlockSpec((None, k, tn), lambda i, j: (layer, 0, j)),
        ],
        out_specs=pl.BlockSpec((tm, tn), lambda i, j: (i, j)),
        out_shape=jax.ShapeDtypeStruct((m, n), F32),
        compiler_params=_cparams(("parallel", "arbitrary")),
        name="in_proj",
    )(h, w_all)


def _out_proj_kernel(a_ref, w_ref, x_ref, gt_ref, o_ref, *, rows_per_batch, tm):
    if rows_per_batch is None:
        row = 0
    else:
        row = 1 + (pl.program_id(0) * tm) // rows_per_batch
    y = jnp.dot(a_ref[...], w_ref[...], preferred_element_type=F32)
    o_ref[...] = x_ref[...] + gt_ref[pl.ds(row, 1), :] * y


def _out_proj(a, w_all, layer, x, gt8, rows_per_batch):
    m, k = a.shape
    n = w_all.shape[2]
    tm = _pick_tile(m if rows_per_batch is None else rows_per_batch, 256, 1024)
    assert m % tm == 0
    tn = _pick_tile(n, 256, 1024)
    return pl.pallas_call(
        functools.partial(_out_proj_kernel, rows_per_batch=rows_per_batch, tm=tm),
        grid=(m // tm, n // tn),
        in_specs=[
            pl.BlockSpec((tm, k), lambda i, j: (i, 0)),
            pl.BlockSpec((None, k, tn), lambda i, j: (layer, 0, j)),
            pl.BlockSpec((tm, tn), lambda i, j: (i, j)),
            pl.BlockSpec((8, tn), lambda i, j: (0, j)),
        ],
        out_specs=pl.BlockSpec((tm, tn), lambda i, j: (i, j)),
        out_shape=jax.ShapeDtypeStruct((m, n), F32),
        compiler_params=_cparams(("parallel", "arbitrary")),
        name="out_proj",
    )(a, w_all, x, gt8)


def _shift_sources(seg_start, seg_width, d_shift, grid_mode):
    if grid_mode:
        q = d_shift // 4
        bounds = [(-1, 0, q), (1, q, 2 * q), (-GRID_W, 2 * q, 3 * q), (GRID_W, 3 * q, d_shift)]
    else:
        half = d_shift // 2
        bounds = [(-1, 0, half), (1, half, d_shift)]
    out = []
    for off, lo, hi in bounds:
        if lo < seg_start + seg_width and hi > seg_start:
            out.append((off, lo, hi))
    return out


def _prep_kernel(r_c, r_p, r_n, k_c, k_p, k_n, v_c, v_p, v_n, l_c, l_p, l_n,
                 mu_r, mu_k, mu_v, mu_l, w0_ref, a0_ref, w2w_ref, w2a_ref,
                 kk_ref, ka_ref, rk_ref, bd_ref,
                 r_o, vx_o, kap_o, bon_o, w_o, kt_o, be_o,
                 rbuf, kbuf, vbuf, lbuf, *, tt, dr, d_shift, grid_mode, halo):
    i = pl.program_id(1)
    nt = pl.num_programs(1)
    first = i == 0
    last = i == nt - 1

    def fill(buf, cur, prv, nxt):
        buf[0:halo, :] = jnp.where(first, 0.0, prv[...])
        buf[halo:halo + tt, :] = cur[...]
        buf[halo + tt:halo + tt + halo, :] = jnp.where(last, 0.0, nxt[...])

    fill(rbuf, r_c, r_p, r_n)
    fill(kbuf, k_c, k_p, k_n)
    fill(vbuf, v_c, v_p, v_n)
    fill(lbuf, l_c, l_p, l_n)

    rowcol = lax.broadcasted_iota(jnp.int32, (tt, 1), 0) % GRID_W

    def shifted(buf, mu_ref, seg_start, c0, cw):
        u = buf[halo:halo + tt, c0:c0 + cw]
        ch = seg_start + c0 + lax.broadcasted_iota(jnp.int32, (1, cw), 1)
        s = jnp.zeros_like(u)
        for off, lo, hi in _shift_sources(seg_start + c0, cw, d_shift, grid_mode):
            nb = buf[halo + off:halo + off + tt, c0:c0 + cw]
            ok = (ch >= lo) & (ch < hi)
            if grid_mode and off == -1:
                ok = ok & (rowcol != 0)
            if grid_mode and off == 1:
                ok = ok & (rowcol != GRID_W - 1)
            s = jnp.where(ok, nb, s)
        return u + mu_ref[:, c0:c0 + cw] * (s - u)

    lo_ = shifted(lbuf, mu_l, 3 * dr, 0, LORA_PAD)
    tw = jnp.tanh(lo_).astype(BF16)
    lo_b = lo_.astype(BF16)
    bd = bd_ref[...].astype(F32)

    lane128 = lax.broadcasted_iota(jnp.int32, (HEAD_DIM, 128), 1)

    for g in range(dr // GROUP_W):
        c0 = g * GROUP_W
        sl = slice(c0, c0 + GROUP_W)
        r = shifted(rbuf, mu_r, 0, c0, GROUP_W)
        k = shifted(kbuf, mu_k, dr, c0, GROUP_W)
        v = shifted(vbuf, mu_v, 2 * dr, c0, GROUP_W)
        kk = k * kk_ref[:, sl]
        ss = jnp.dot(kk * kk, bd, preferred_element_type=F32, precision=lax.Precision.HIGHEST)
        kap = kk * lax.rsqrt(jnp.maximum(ss, 1e-24))
        r_o[:, sl] = r
        kap_o[:, sl] = kap
        for q in range(GROUP_W // 128):
            vt = v[:, q * 128:(q + 1) * 128].T
            first, second = vt[:HEAD_DIM], vt[HEAD_DIM:]
            pair = g * (GROUP_W // 128) + q
            vx_o[0, pair] = jnp.where(lane128 < VX_STEPS, first, pltpu.roll(second, VX_STEPS, axis=1))
            vx_o[1, pair] = jnp.where(lane128 < VX_STEPS, pltpu.roll(first, VX_STEPS, axis=1), second)
        ktsum = jnp.zeros_like(k)
        for d in range(2):
            wl = w0_ref[d:d + 1, sl] + jnp.dot(tw, w2w_ref[d, :, sl], preferred_element_type=F32)
            w_o[d, :, sl] = jnp.exp(-(EXP_M05 * jax.nn.sigmoid(wl)))
            a = jax.nn.sigmoid(a0_ref[d:d + 1, sl]
                               + jnp.dot(lo_b, w2a_ref[d, :, sl], preferred_element_type=F32))
            kt = k * (1.0 + (a - 1.0) * ka_ref[:, sl])
            kt_o[d, :, sl] = kt
            be_o[d, :, sl] = kap * a
            ktsum = ktsum + kt
        cs = jnp.dot(r * ktsum * rk_ref[:, sl], bd, preferred_element_type=F32,
                     precision=lax.Precision.HIGHEST)
        bon_o[:, sl] = cs * v


def _prep(proj, lw, grid_mode):
    b, t, dp = proj.shape
    dr = lw["dr"]
    d_shift = lw["d_shift"]
    halo = GRID_W if grid_mode else 8
    tt = 2 * VX_STEPS
    assert tt % halo == 0 and t % tt == 0
    nh = tt // halo
    n_halo_blocks = t // halo
    lora_blk = (7 * dr) // LORA_PAD

    def seg_specs(col_blk, width):
        return [
            pl.BlockSpec((None, tt, width), lambda i, j: (i, j, col_blk)),
            pl.BlockSpec((None, halo, width),
                         lambda i, j: (i, jnp.maximum(j * nh - 1, 0), col_blk)),
            pl.BlockSpec((None, halo, width),
                         lambda i, j: (i, jnp.minimum((j + 1) * nh, n_halo_blocks - 1), col_blk)),
        ]

    def const(shape):
        nd = len(shape)
        return pl.BlockSpec(shape, lambda i, j: (0,) * nd)

    in_specs = (seg_specs(0, dr) + seg_specs(1, dr) + seg_specs(2, dr) + seg_specs(lora_blk, LORA_PAD)
                + [const((1, dr)), const((1, dr)), const((1, dr)), const((1, LORA_PAD)),
                   const((2, dr)), const((2, dr)), const((2, LORA_PAD, dr)), const((2, LORA_PAD, dr)),
                   const((1, dr)), const((1, dr)), const((1, dr)), const((GROUP_W, GROUP_W))])
    tok = pl.BlockSpec((None, tt, dr), lambda i, j: (i, j, 0))
    tok2 = pl.BlockSpec((2, None, tt, dr), lambda i, j: (0, i, j, 0))
    sd = jax.ShapeDtypeStruct((b, t, dr), F32)
    sd2 = jax.ShapeDtypeStruct((2, b, t, dr), F32)
    args = [proj] * 12 + [lw["mu_r"], lw["mu_k"], lw["mu_v"], lw["mu_l"], lw["w0"], lw["a0"],
                          lw["w2w"], lw["w2a"], lw["k_k"], lw["k_a"], lw["r_k"], lw["bd"]]
    return pl.pallas_call(
        functools.partial(_prep_kernel, tt=tt, dr=dr, d_shift=d_shift, grid_mode=grid_mode, halo=halo),
        grid=(b, t // tt),
        in_specs=in_specs,
        out_specs=[tok,
                   pl.BlockSpec((None, tt // VX_STEPS, dr // 128, HEAD_DIM, 128),
                                lambda i, j: (i, j, 0, 0, 0)),
                   tok, tok, tok2, tok2, tok2],
        out_shape=[sd, jax.ShapeDtypeStruct((b, t // VX_STEPS, dr // 128, HEAD_DIM, 128), F32),
                   sd, sd, sd2, sd2, sd2],
        scratch_shapes=[pltpu.VMEM((tt + 2 * halo, dr), F32)] * 3
                       + [pltpu.VMEM((tt + 2 * halo, LORA_PAD), F32)],
        compiler_params=_cparams(("parallel", "parallel")),
        name="prep",
    )(*args)


def _scan_kernel(kap_ref, r_ref, vx_ref, w_ref, kt_ref, be_ref, s0_ref, bd_ref, e_ref,
                 o_ref, sf_ref, st, lhs, qbuf, oacc, *, ng, nb):
    d = pl.program_id(1)
    c = pl.program_id(2)

    @pl.when(c == 0)
    def _():
        st[...] = s0_ref[...]
        qbuf[...] = jnp.zeros_like(qbuf)

    oacc[...] = jnp.zeros_like(oacc)
    bd = bd_ref[...]
    tiles = [(s, g, (s * ng + g) * HEAD_DIM, slice(g * GROUP_W, (g + 1) * GROUP_W))
             for s in range(nb) for g in range(ng)]

    def column(j):
        return j + d * (OUT_BLK - 1 - 2 * j)

    def reduce_outputs(ecol):
        oc = jnp.dot(qbuf[...], e_ref[ecol], preferred_element_type=F32)
        for s, g, r0, _ in tiles:
            oacc[s, g] += oc[r0:r0 + HEAD_DIM]

    chunk = c + d * (pl.num_programs(2) - 1 - 2 * c)
    half = (chunk % (VX_STEPS // OUT_BLK)) * OUT_BLK
    lane = lax.broadcasted_iota(jnp.int32, (HEAD_DIM, 128), 1)

    def row(ref, s, t0, off, sl):
        return ref[s, pl.ds(t0, 8), sl][off:off + 1, :]

    t_first = column(0)
    for s, g, r0, sl in tiles:
        lhs[r0:r0 + HEAD_DIM, :] = st[s, g].astype(BF16) * kap_ref[s, pl.ds(t_first, 1), sl].astype(BF16)

    def eight_steps(blk, backward):
        n_tiles = OUT_BLK // 8
        t0 = pl.multiple_of(((n_tiles - 1 - blk) if backward else blk) * 8, 8)
        nxt = jnp.minimum(blk + 1, n_tiles - 1)
        t0_next = pl.multiple_of(((n_tiles - 1 - nxt) if backward else nxt) * 8, 8)
        for j in range(8):
            off = 7 - j if backward else j
            i = blk * 8 + j
            t = t0 + off
            if j < 7:
                tn0, off_next = t0, (off - 1 if backward else off + 1)
            else:
                tn0, off_next = t0_next, (7 if backward else 0)
            idx = jnp.where(lane < HEAD_DIM, half + t, VX_STEPS + half + t)
            res = jnp.dot(lhs[...], bd, preferred_element_type=F32)
            prev_col = (OUT_BLK - i) if backward else (i - 1)
            reduce_outputs(jnp.where(i == 0, OUT_BLK, prev_col))
            for s, g, r0, sl in tiles:
                s_new = (st[s, g] * row(w_ref, s, t0, off, sl)
                         - res[r0:r0 + HEAD_DIM] * row(be_ref, s, t0, off, sl)
                         + jnp.concatenate(
                             [jnp.take_along_axis(vx_ref[s, 2 * g + q], idx, axis=1) for q in range(2)],
                             axis=1) * row(kt_ref, s, t0, off, sl))
                st[s, g] = s_new
                sb = s_new.astype(BF16)
                qbuf[r0:r0 + HEAD_DIM, :] = sb * row(r_ref, s, t0, off, sl).astype(BF16)
                lhs[r0:r0 + HEAD_DIM, :] = sb * row(kap_ref, s, tn0, off_next, sl).astype(BF16)

    def block(blk, carry):
        lax.cond(d == 0, lambda: eight_steps(blk, False), lambda: eight_steps(blk, True))
        return carry

    lax.fori_loop(0, OUT_BLK // 8, block, 0)
    reduce_outputs(column(OUT_BLK - 1))
    for s, g, _, _ in tiles:
        ot = oacc[s, g].T
        for h in range(GROUP_HEADS):
            o_ref[s, g * GROUP_HEADS + h] = ot[h * OUT_BLK:(h + 1) * OUT_BLK, :]

    @pl.when(c == pl.num_programs(2) - 1)
    def _():
        sf_ref[...] = st[...]


def _scan(kap, r, vx, w, kt, be, s0, bd, e_all):
    b, t, dr = kap.shape
    ng = dr // GROUP_W
    h = dr // HEAD_DIM
    tc = OUT_BLK
    assert t % tc == 0
    nt = t // tc
    nb = _pick_tile(b, 1, SCAN_SEQS)
    rows = nb * ng * HEAD_DIM

    def cidx(dd, cc):
        return cc + dd * (nt - 1 - 2 * cc)

    tok = pl.BlockSpec((nb, tc, dr), lambda i, dd, cc: (i, cidx(dd, cc), 0))
    tok2 = pl.BlockSpec((None, nb, tc, dr), lambda i, dd, cc: (dd, i, cidx(dd, cc), 0))
    state = pl.BlockSpec((nb, None, ng, HEAD_DIM, GROUP_W), lambda i, dd, cc: (i, dd, 0, 0, 0))
    vxs = pl.BlockSpec((nb, None, 2 * ng, HEAD_DIM, 128),
                       lambda i, dd, cc: (i, cidx(dd, cc) // (VX_STEPS // OUT_BLK), 0, 0, 0))
    return pl.pallas_call(
        functools.partial(_scan_kernel, ng=ng, nb=nb),
        grid=(b // nb, 2, nt),
        in_specs=[tok, tok, vxs, tok2, tok2, tok2, state,
                  pl.BlockSpec((GROUP_W, GROUP_W), lambda i, dd, cc: (0, 0)),
                  pl.BlockSpec((OUT_BLK + 1, GROUP_W, 128), lambda i, dd, cc: (0, 0, 0))],
        out_specs=[
            pl.BlockSpec((None, nb, h, tc, HEAD_DIM), lambda i, dd, cc: (dd, i, 0, cidx(dd, cc), 0)),
            state,
        ],
        out_shape=[jax.ShapeDtypeStruct((2, b, h, t, HEAD_DIM), F32),
                   jax.ShapeDtypeStruct((b, 2, ng, HEAD_DIM, GROUP_W), F32)],
        scratch_shapes=[pltpu.VMEM((nb, ng, HEAD_DIM, GROUP_W), F32),
                        pltpu.VMEM((rows, GROUP_W), BF16),
                        pltpu.VMEM((rows, GROUP_W), BF16),
                        pltpu.VMEM((nb, ng, HEAD_DIM, 128), F32)],
        compiler_params=_cparams(("parallel", "parallel", "arbitrary")),
        name="rwkv7_scan",
    )(kap, r, vx, w, kt, be, s0, bd, e_all)


def _post_kernel(o_ref, bon_ref, g_ref, p_c, p_p, p_n, q_c, q_p, q_n, gc_ref,
                 lng_ref, lnb_ref, cw_ref, cb_ref, clg_ref, clb_ref,
                 x_o, zbuf, cbuf, zsh, gn, *, tt, dr, kc, halo):
    i = pl.program_id(1)
    first = i == 0
    last = i == pl.num_programs(1) - 1
    pad = kc // 2

    n_heads = dr // HEAD_DIM
    for h in range(n_heads):
        o = o_ref[0, h] + o_ref[1, h]
        gn[0, h] = o
        gn[1, h] = jnp.broadcast_to(jnp.mean(o, axis=-1, keepdims=True), o.shape)
    for h in range(n_heads):
        dev = gn[0, h] - gn[1, h]
        var = jnp.mean(jnp.square(dev), axis=-1, keepdims=True)
        gn[2, h] = jnp.broadcast_to(lax.rsqrt(var + GN_EPS), dev.shape)

    def normed(h):
        return (gn[0, h] - gn[1, h]) * gn[2, h]

    for hp in range(dr // 128):
        sl = slice(hp * 128, (hp + 1) * 128)
        on = jnp.concatenate([normed(2 * hp), normed(2 * hp + 1)], axis=-1)
        xr = (on * lng_ref[:, sl] + lnb_ref[:, sl] + bon_ref[:, sl]) * _silu(g_ref[:, sl])
        x_o[:, sl] = xr.astype(x_o.dtype)

    def glu(p, q):
        return p * jax.nn.sigmoid(q)

    zbuf[0:halo, :] = jnp.where(first, 0.0, glu(p_p[...], q_p[...]))
    zbuf[halo:halo + tt, :] = glu(p_c[...], q_c[...])
    zbuf[halo + tt:halo + tt + halo, :] = jnp.where(last, 0.0, glu(p_n[...], q_n[...]))

    first_start = halo - pad
    n_rows = zsh.shape[1]
    for sh in range(8):
        zsh[sh] = zbuf[sh:sh + n_rows, :]
    for cg in range(dr // 128):
        sl = slice(cg * 128, (cg + 1) * 128)
        accs = [jnp.zeros((tt, 128), F32), jnp.zeros((tt, 128), F32)]
        for j in range(kc):
            start = first_start + j
            base = (start // 8) * 8
            accs[j % 2] = accs[j % 2] + zsh[start % 8, base:base + tt, sl] * cw_ref[j:j + 1, sl]
        cbuf[:, sl] = accs[0] + accs[1] + cb_ref[:, sl]

    z = cbuf[...]
    zm = jnp.mean(z, axis=-1, keepdims=True)
    zv = jnp.mean(jnp.square(z - zm), axis=-1, keepdims=True)
    zn = (z - zm) * lax.rsqrt(zv + LN_EPS) * clg_ref[...] + clb_ref[...]
    x_o[:, dr:] = (_silu(zn) * _silu(gc_ref[...])).astype(x_o.dtype)


def _post(o, bonus, proj, lw):
    _, b, h, t, _ = o.shape
    dr = lw["dr"]
    kc = lw["conv_w"].shape[0]
    halo = 16
    assert kc // 2 <= halo
    tt = min(128, t)
    nh = tt // halo
    n_halo_blocks = t // halo

    def seg(col_blk):
        return pl.BlockSpec((None, tt, dr), lambda i, j: (i, j, col_blk))

    def seg_halo(col_blk):
        return [
            seg(col_blk),
            pl.BlockSpec((None, halo, dr), lambda i, j: (i, jnp.maximum(j * nh - 1, 0), col_blk)),
            pl.BlockSpec((None, halo, dr),
                         lambda i, j: (i, jnp.minimum((j + 1) * nh, n_halo_blocks - 1), col_blk)),
        ]

    def const(shape):
        nd = len(shape)
        return pl.BlockSpec(shape, lambda i, j: (0,) * nd)

    in_specs = ([pl.BlockSpec((2, None, h, tt, HEAD_DIM), lambda i, j: (0, i, 0, j, 0)),
                 pl.BlockSpec((None, tt, dr), lambda i, j: (i, j, 0)),
                 seg(3)] + seg_halo(4) + seg_halo(5) + [seg(6)]
                + [const((1, dr)), const((1, dr)), const((kc, dr)), const((1, dr)),
                   const((1, dr)), const((1, dr))])
    args = [o, bonus] + [proj] * 8 + [lw["lnx_g"], lw["lnx_b"], lw["conv_w"], lw["conv_b"],
                                      lw["cln_g"], lw["cln_b"]]
    return pl.pallas_call(
        functools.partial(_post_kernel, tt=tt, dr=dr, kc=kc, halo=halo),
        grid=(b, t // tt),
        in_specs=in_specs,
        out_specs=pl.BlockSpec((None, tt, 2 * dr), lambda i, j: (i, j, 0)),
        out_shape=jax.ShapeDtypeStruct((b, t, 2 * dr), BF16),
        scratch_shapes=[pltpu.VMEM((tt + 2 * halo, dr), F32), pltpu.VMEM((tt, dr), F32),
                        pltpu.VMEM((8, tt + 2 * halo - 8, dr), F32),
                        pltpu.VMEM((3, h, tt, HEAD_DIM), F32)],
        compiler_params=_cparams(("parallel", "parallel")),
        name="post",
    )(*args)


def _block_diag_ones():
    idx = np.arange(GROUP_W) // HEAD_DIM
    return jnp.asarray((idx[:, None] == idx[None, :]).astype(np.float32), dtype=BF16)


def _column_selectors():
    e = np.zeros((OUT_BLK + 1, GROUP_W, 128), np.float32)
    l = np.arange(GROUP_W)
    for j in range(OUT_BLK):
        e[j, l, (l // HEAD_DIM) * OUT_BLK + j] = 1.0
    return jnp.asarray(e, dtype=BF16)


def _projection_weights(w_in, w_out, dr, d_shift):
    lora_cols = jnp.pad(w_in[:, :, 3 * dr:d_shift], ((0, 0), (0, 0), (0, LORA_PAD - (d_shift - 3 * dr))))
    w_in_all = jnp.concatenate([w_in[:, :, :3 * dr], w_in[:, :, d_shift:], lora_cols], axis=2)
    return w_in_all.astype(BF16), w_out.astype(BF16)


def _layer_weights(l, w_in, mu_shift, decay_w0, decay_w2, iclr_a0, iclr_a2, k_k, k_a, r_k,
                   lnx_g, lnx_b, conv_w, conv_b, cln_g, cln_b, bd):
    d = w_in.shape[1]
    dr = k_k.shape[1]
    lw_ = decay_w2.shape[2]
    la_ = iclr_a2.shape[2]
    assert lw_ + la_ <= LORA_PAD and conv_w.shape[2] == dr and d == 2 * dr
    d_shift = 3 * dr + lw_ + la_
    mu = mu_shift[l]
    w2w = jnp.zeros((2, LORA_PAD, dr), F32).at[:, :lw_].set(decay_w2[l]).astype(BF16)
    w2a = jnp.zeros((2, LORA_PAD, dr), F32).at[:, lw_:lw_ + la_].set(iclr_a2[l]).astype(BF16)
    row = lambda x: x.reshape(1, -1)
    return dict(
        layer=l, dr=dr, d_shift=d_shift,
        mu_r=row(mu[:dr]), mu_k=row(mu[dr:2 * dr]), mu_v=row(mu[2 * dr:3 * dr]),
        mu_l=row(jnp.pad(mu[3 * dr:], (0, LORA_PAD - lw_ - la_))),
        w0=decay_w0[l], a0=iclr_a0[l], w2w=w2w, w2a=w2a,
        k_k=row(k_k[l]), k_a=row(k_a[l]), r_k=row(r_k[l]),
        lnx_g=row(lnx_g[l]), lnx_b=row(lnx_b[l]), conv_w=conv_w[l], conv_b=row(conv_b[l]),
        cln_g=row(cln_g[l]), cln_b=row(cln_b[l]), bd=bd)


def _to_tiles(s):
    b, two, h, n, _ = s.shape
    s = s.reshape(b, two, h // GROUP_HEADS, GROUP_HEADS, n, n)
    return jnp.transpose(s, (0, 1, 2, 4, 3, 5)).reshape(b, two, h // GROUP_HEADS, n, GROUP_W)


def _from_tiles(s):
    b, two, ng, n, _ = s.shape
    s = s.reshape(b, two, ng, n, GROUP_HEADS, n)
    return jnp.transpose(s, (0, 1, 2, 4, 3, 5)).reshape(b, two, ng * GROUP_HEADS, n, n)


def _mixer_layer(x, mod, s0_tiles, lw, w_in_all, w_out_all, norm_g, grid_mode, per_batch_row, e_all):
    b, t, d = x.shape
    h = _norm_mod(x, norm_g, mod, per_batch_row)
    proj = _in_proj(h.reshape(b * t, d), w_in_all, lw["layer"]).reshape(b, t, -1)
    r, vx, kap, bonus, w, kt, be = _prep(proj, lw, grid_mode)
    o, s_fin = _scan(kap, r, vx, w, kt, be, s0_tiles, lw["bd"], e_all)
    xcat = _post(o, bonus, proj, lw)
    x_new = _out_proj(xcat.reshape(b * t, d), w_out_all, lw["layer"], x.reshape(b * t, d),
                      mod[:, 2 * d:], t if per_batch_row else None).reshape(b, t, d)
    return x_new, s_fin


def kernel(x_prompt, x_sample, state_rwkv, c, c_ctx, norm_g, ada_w, ada_b, w_in, mu_shift,
           decay_w0, decay_w2, iclr_a0, iclr_a2, k_k, k_a, r_k, lnx_g, lnx_b, conv_w, conv_b,
           cln_g, cln_b, w_out, final_g):
    depth = w_in.shape[0]
    d = x_prompt.shape[-1]
    nb_ctx = x_prompt.shape[0]
    nb_lat = x_sample.shape[0]
    assert nb_lat + 1 <= 8
    dr = k_k.shape[1]
    hh = dr // HEAD_DIM

    cond8 = jnp.zeros((8, d), F32).at[0].set(c_ctx).at[1:1 + nb_lat].set(c)
    mod = _modulation(cond8, ada_w, ada_b)

    bd = _block_diag_ones()
    e_all = _column_selectors()
    zeros_state = jnp.zeros((nb_ctx, 2, hh // GROUP_HEADS, HEAD_DIM, GROUP_W), F32)
    r_k_flat = r_k.reshape(depth, dr)
    d_shift = 3 * dr + decay_w2.shape[2] + iclr_a2.shape[2]
    w_in_all, w_out_all = _projection_weights(w_in, w_out, dr, d_shift)

    x_ctx, x_lat = x_prompt, x_sample
    new_states = []
    for l in range(depth):
        lw = _layer_weights(l, w_in, mu_shift, decay_w0, decay_w2, iclr_a0, iclr_a2, k_k, k_a,
                            r_k_flat, lnx_g, lnx_b, conv_w, conv_b, cln_g, cln_b, bd)
        x_ctx, s_fin = _mixer_layer(x_ctx, mod[l], zeros_state, lw, w_in_all, w_out_all, norm_g[l],
                                    False, False, e_all)
        new_states.append(_from_tiles(s_fin))
        x_lat, _ = _mixer_layer(x_lat, mod[l], _to_tiles(state_rwkv[:, l]), lw, w_in_all, w_out_all,
                                norm_g[l], True, True, e_all)
    y_prompt = _final_norm(x_ctx, final_g)
    y_sample = _final_norm(x_lat, final_g)
    return (y_prompt, y_sample, jnp.stack(new_states, axis=1))
```
